```python
import jax, jax.numpy as jnp
from jax import lax

D_MODEL = 1024
BATCH = 8
SEQ = 8192
DEPTH = 1

D_CONV = D_MODEL // 2
CONV_GROUPS = 8
CONV_WIDTH = 3
N_HEADS = 8
HEAD_DIM = 64
N_KV = 2
HPG = N_HEADS // N_KV
D_ATTN = N_HEADS * HEAD_DIM
D_MIX = D_CONV + D_ATTN
KV_DIM = N_KV * HEAD_DIM
CMP_LEN = 32
CMP_STRIDE = 16
CMP_HIDDEN = 2 * HEAD_DIM
SEL_LEN = 64
TOP_N = 16
WINDOW = 512
Q_BLOCK = 128
D_FF = -(-8 * D_MODEL // (3 * 256)) * 256
D_IN = 3 * D_CONV + D_ATTN + 6 * KV_DIM + 3 * N_HEADS
NEG = -1e30
FORCE = 1e9

kernel_name = "hybrid_conv_nsa_sandwich_adaln"


def rmsnorm(x, g, eps=1e-6):
    xf = x.astype(jnp.float32)
    y = xf * lax.rsqrt(jnp.mean(xf * xf, axis=-1, keepdims=True) + eps)
    return (y * g.astype(jnp.float32)).astype(x.dtype)


def alibi_slopes():
    h = jnp.arange(N_HEADS, dtype=jnp.float32) + 1.0
    return (2.0 ** (-8.0 * h / N_HEADS)).reshape(N_KV, HPG)


def short_conv_mixer(u, conv_w):
    b_gate, c_gate, v = jnp.split(u, 3, axis=-1)
    z = c_gate * v
    rhs = conv_w.astype(z.dtype)[:, None, :]
    zc = lax.conv_general_dilated(z, rhs, window_strides=(1,), padding=[(CONV_WIDTH - 1, 0)],
                                  dimension_numbers=("NWC", "WIO", "NWC"), feature_group_count=D_CONV)
    return b_gate * zc


def compress(kv, pos, w1, w2):
    B, S = kv.shape[:2]
    ch = kv.reshape(B, S // CMP_STRIDE, CMP_STRIDE, N_KV, HEAD_DIM)
    blk = jnp.concatenate([ch[:, :-1], ch[:, 1:]], axis=2) + pos.astype(kv.dtype)[None, None, :, None, :]
    h = jax.nn.gelu(jnp.einsum("bclgd,ldh->bcgh", blk, w1))
    return jnp.einsum("bcgh,hd->bcgd", h, w2)


def gather_blocks(blocks, idx):
    return jax.vmap(jax.vmap(lambda b_, i_: b_[i_]))(blocks, idx)


def nsa_mixer(q, k_cmp, v_cmp, k_sel, v_sel, k_win, v_win, gates, pos_cmp, w_ck1, w_ck2, w_cv1, w_cv2):
    B, S = q.shape[:2]
    f32 = jnp.float32
    kc = compress(k_cmp, pos_cmp, w_ck1, w_ck2)
    vc = compress(v_cmp, pos_cmp, w_cv1, w_cv2)
    NC = kc.shape[1]
    NS = S // SEL_LEN
    top_n = min(TOP_N, NS)
    ksb = k_sel.reshape(B, NS, SEL_LEN, N_KV, HEAD_DIM).transpose(0, 3, 1, 2, 4)
    vsb = v_sel.reshape(B, NS, SEL_LEN, N_KV, HEAD_DIM).transpose(0, 3, 1, 2, 4)
    pad = ((0, 0), (WINDOW, 0), (0, 0), (0, 0))
    kw_pad = jnp.pad(k_win, pad)
    vw_pad = jnp.pad(v_win, pad)
    sl = alibi_slopes()
    sl5 = sl[None, :, :, None, None]
    cmp_start = jnp.arange(NC) * CMP_STRIDE
    cmp_end = cmp_start + CMP_LEN - 1
    sel_start = jnp.arange(NS) * SEL_LEN
    overlap = ((cmp_start[:, None] < sel_start[None, :] + SEL_LEN) &
               (cmp_start[:, None] + CMP_LEN > sel_start[None, :])).astype(f32)
    scale = HEAD_DIM ** -0.5
    nq = S // Q_BLOCK
    qb = q.reshape(B, nq, Q_BLOCK, N_KV, HPG, HEAD_DIM).transpose(1, 0, 2, 3, 4, 5)
    gb = gates.reshape(B, nq, Q_BLOCK, N_KV, HPG, 3).transpose(1, 0, 2, 3, 4, 5)
    blk_ids = jnp.arange(NS)

    def block(args):
        q_c, g_c, q0 = args
        t = q0 + jnp.arange(Q_BLOCK)
        s = jnp.einsum("bqgnd,bkgd->bgnqk", q_c, kc, preferred_element_type=f32) * scale
        valid = cmp_end[None, :] <= t[:, None]
        dist = (t[:, None] - cmp_end[None, :]).astype(f32)
        s = jnp.where(valid, s - sl5 * dist, NEG)
        p = jax.nn.softmax(s, axis=-1) * jnp.any(valid, axis=-1)[:, None].astype(f32)
        o_cmp = jnp.einsum("bgnqk,bkgd->bqgnd", p.astype(vc.dtype), vc)
        imp = jnp.einsum("bgnqk,ks->bgqs", p, overlap)
        cur = t // SEL_LEN
        forced = (blk_ids[None, :] == 0) | (blk_ids[None, :] == cur[:, None]) | (blk_ids[None, :] == cur[:, None] - 1)
        causal_blk = sel_start[None, :] <= t[:, None]
        imp = jnp.where(causal_blk, jnp.where(forced, FORCE, imp), NEG)
        top_val, top_idx = lax.top_k(imp, top_n)
        kg = gather_blocks(ksb, top_idx)
        vg = gather_blocks(vsb, top_idx)
        kpos = top_idx[..., None] * SEL_LEN + jnp.arange(SEL_LEN)
        ok = (top_val > NEG / 2)[..., None] & (kpos <= t[:, None, None])
        s = jnp.einsum("bqgnd,bgqkld->bgnqkl", q_c, kg, preferred_element_type=f32) * scale
        dist = (t[:, None, None] - kpos).astype(f32)[:, :, None]
        s = jnp.where(ok[:, :, None], s - sl[None, :, :, None, None, None] * dist, NEG)
        p = jax.nn.softmax(s, axis=(-2, -1))
        o_sel = jnp.einsum("bgnqkl,bgqkld->bqgnd", p.astype(vg.dtype), vg)
        kw = lax.dynamic_slice_in_dim(kw_pad, q0, Q_BLOCK + WINDOW, axis=1)
        vw = lax.dynamic_slice_in_dim(vw_pad, q0, Q_BLOCK + WINDOW, axis=1)
        wpos = q0 - WINDOW + jnp.arange(Q_BLOCK + WINDOW)
        d = t[:, None] - wpos[None, :]
        okw = (d >= 0) & (d < WINDOW) & (wpos >= 0)[None, :]
        s = jnp.einsum("bqgnd,bkgd->bgnqk", q_c, kw, preferred_element_type=f32) * scale
        s = jnp.where(okw, s - sl5 * d.astype(f32), NEG)
        p = jax.nn.softmax(s, axis=-1)
        o_win = jnp.einsum("bgnqk,bkgd->bqgnd", p.astype(vw.dtype), vw)
        return g_c[..., 0:1] * o_cmp + g_c[..., 1:2] * o_sel + g_c[..., 2:3] * o_win

    out = lax.map(block, (qb, gb, jnp.arange(nq) * Q_BLOCK))
    return out.transpose(1, 0, 2, 3, 4, 5).reshape(B, S, N_KV, HPG, HEAD_DIM)


def setup_inputs(seed: int = 0) -> dict:
    key = jax.random.key(seed)
    ks = jax.random.split(key, 24)
    f32 = jnp.float32
    L = DEPTH

    def nrm(k, shape, s):
        return jax.random.normal(k, shape, f32) * s

    def gain(k, shape):
        return 1.0 + 0.02 * jax.random.normal(k, shape, f32)

    return {
        "x": nrm(ks[0], (BATCH, SEQ, D_MODEL), 1.0),
        "c": nrm(ks[1], (BATCH, D_MODEL), 1.0),
        "w_mod": nrm(ks[2], (L, D_MODEL, 6 * D_MODEL), 0.5 * D_MODEL ** -0.5),
        "b_mod": nrm(ks[3], (L, 6 * D_MODEL), 0.01),
        "g_pre_mix": gain(ks[4], (L, D_MODEL)),
        "w_in": nrm(ks[5], (L, D_MODEL, D_IN), D_MODEL ** -0.5),
        "conv_w": nrm(ks[6], (L, CONV_WIDTH, D_CONV), CONV_WIDTH ** -0.5),
        "pos_cmp": nrm(ks[7], (L, CMP_LEN, HEAD_DIM), 0.1),
        "w_ck1": nrm(ks[8], (L, CMP_LEN, HEAD_DIM, CMP_HIDDEN), (CMP_LEN * HEAD_DIM) ** -0.5),
        "w_ck2": nrm(ks[9], (L, CMP_HIDDEN, HEAD_DIM), CMP_HIDDEN ** -0.5),
        "w_cv1": nrm(ks[10], (L, CMP_LEN, HEAD_DIM, CMP_HIDDEN), (CMP_LEN * HEAD_DIM) ** -0.5),
        "w_cv2": nrm(ks[11], (L, CMP_HIDDEN, HEAD_DIM), CMP_HIDDEN ** -0.5),
        "g_conv_out": gain(ks[12], (L, D_CONV)),
        "g_attn_out": gain(ks[13], (L, D_ATTN)),
        "w_out": nrm(ks[14], (L, D_MIX, D_MODEL), D_MIX ** -0.5),
        "g_post_mix": gain(ks[15], (L, D_MODEL)),
        "g_pre_ffn": gain(ks[16], (L, D_MODEL)),
        "w_gate": nrm(ks[17], (L, D_MODEL, D_FF), D_MODEL ** -0.5),
        "w_up": nrm(ks[18], (L, D_MODEL, D_FF), D_MODEL ** -0.5),
        "w_down": nrm(ks[19], (L, D_FF, D_MODEL), D_FF ** -0.5),
        "g_post_ffn": gain(ks[20], (L, D_MODEL)),
    }


def reference(x, c, w_mod, b_mod, g_pre_mix, w_in, conv_w, pos_cmp, w_ck1, w_ck2, w_cv1, w_cv2,
              g_conv_out, g_attn_out, w_out, g_post_mix, g_pre_ffn, w_gate, w_up, w_down, g_post_ffn):
    B, S, _ = x.shape
    splits = [3 * D_CONV, 3 * D_CONV + D_ATTN, 3 * D_CONV + D_ATTN + 6 * KV_DIM]
    for l in range(DEPTH):
        mod = (jax.nn.silu(c) @ w_mod[l] + b_mod[l])[:, None, :]
        sh_m, sc_m, ga_m, sh_f, sc_f, ga_f = jnp.split(mod, 6, axis=-1)
        h = rmsnorm(x, g_pre_mix[l]) * (1.0 + sc_m) + sh_m
        u = h @ w_in[l]
        u_conv, q, kv, gl = jnp.split(u, splits, axis=-1)
        y_conv = rmsnorm(short_conv_mixer(u_conv, conv_w[l]), g_conv_out[l])
        q = q.reshape(B, S, N_KV, HPG, HEAD_DIM)
        k_cmp, v_cmp, k_sel, v_sel, k_win, v_win = [a.reshape(B, S, N_KV, HEAD_DIM) for a in jnp.split(kv, 6, axis=-1)]
        gates = jax.nn.sigmoid(gl.astype(jnp.float32)).astype(x.dtype).reshape(B, S, N_KV, HPG, 3)
        y_attn = nsa_mixer(q, k_cmp, v_cmp, k_sel, v_sel, k_win, v_win, gates,
                           pos_cmp[l], w_ck1[l], w_ck2[l], w_cv1[l], w_cv2[l])
        y_attn = rmsnorm(y_attn, g_attn_out[l].reshape(N_KV, HPG, HEAD_DIM)).reshape(B, S, D_ATTN)
        y = jnp.concatenate([y_conv, y_attn], axis=-1) @ w_out[l]
        x = x + ga_m * rmsnorm(y, g_post_mix[l])
        h = rmsnorm(x, g_pre_ffn[l]) * (1.0 + sc_f) + sh_f
        f = (jax.nn.silu(h @ w_gate[l]) * (h @ w_up[l])) @ w_down[l]
        x = x + ga_f * rmsnorm(f, g_post_ffn[l])
    return x
```

```python
import functools
import math

import jax
import jax.numpy as jnp
from jax import lax
from jax.experimental import pallas as pl
from jax.experimental.pallas import tpu as pltpu

F32 = jnp.float32
BF16 = jnp.bfloat16

D_MODEL = 1024
D_CONV = 512
CONV_WIDTH = 3
N_HEADS = 8
HEAD_DIM = 64
N_KV = 2
HPG = N_HEADS // N_KV
D_ATTN = N_HEADS * HEAD_DIM
KV_DIM = N_KV * HEAD_DIM
CMP_LEN = 32
CMP_STRIDE = 16
CMP_HIDDEN = 2 * HEAD_DIM
SEL_LEN = 64
TOP_N = 16
WINDOW = 512
D_FF = 2816
NEG = -1e30
FORCE = 1e9
EPS = 1e-6

LANES = 128
Q_TILE = 128
KV_TILE = 512
WIN_SPAN = WINDOW + Q_TILE
ROW_TILE = 512
FF_CHUNK = 256
VMEM_LIMIT = 56 * 1024 * 1024

SLOPES = [[2.0 ** (-(g * HPG + n + 1)) for n in range(HPG)] for g in range(N_KV)]


def _const_spec(shape):
    nd = len(shape)
    return pl.BlockSpec(shape, lambda *_: (0,) * nd, pipeline_mode=pl.Buffered(1))


def _rms(v):
    return v * lax.rsqrt(jnp.mean(v * v, axis=-1, keepdims=True) + EPS)


def _dot_nt(a, b):
    return lax.dot_general(a, b, (((1,), (1,)), ((), ())), preferred_element_type=F32)


def _mod_kernel(c_ref, w_ref, b_ref, o_ref):
    c = c_ref[...]
    a = c * jax.nn.sigmoid(c)
    o_ref[...] = jnp.dot(a, w_ref[...], preferred_element_type=F32,
                         precision=lax.Precision.HIGHEST) + b_ref[...]


def _modulation(c, w_mod, b_mod):
    bsz = c.shape[0]
    n_out = w_mod.shape[1]
    blk = D_MODEL
    return pl.pallas_call(
        _mod_kernel,
        grid=(n_out // blk,),
        in_specs=[pl.BlockSpec((bsz, D_MODEL), lambda j: (0, 0)),
                  pl.BlockSpec((D_MODEL, blk), lambda j: (0, j)),
                  pl.BlockSpec((1, blk), lambda j: (0, j))],
        out_specs=pl.BlockSpec((bsz, blk), lambda j: (0, j)),
        out_shape=jax.ShapeDtypeStruct((bsz, n_out), F32),
        compiler_params=pltpu.CompilerParams(dimension_semantics=("arbitrary",),
                                             vmem_limit_bytes=VMEM_LIMIT),
        name="adaln_mod",
    )(c, w_mod, b_mod.reshape(1, n_out))


def _inproj_kernel(x_ref, mod_ref, gpre_ref, wconv_ref, wq_ref, wkv_ref, wgl_ref, convw_ref, gconv_ref,
                   yconv_ref, q_ref, cmp_ref, ksel_ref, vsel_ref, kwin_ref, vwin_ref, gates_ref,
                   zbuf_ref):
    tm = x_ref.shape[1]
    x = x_ref[0]
    sh = mod_ref[0, :, 0:D_MODEL]
    sc = mod_ref[0, :, D_MODEL:2 * D_MODEL]
    h = (_rms(x) * gpre_ref[...] * (1.0 + sc) + sh).astype(BF16)

    u = jnp.dot(h, wconv_ref[...], preferred_element_type=F32)
    z = u[:, D_CONV:2 * D_CONV] * u[:, 2 * D_CONV:]

    @pl.when(pl.program_id(1) == 0)
    def _():
        zbuf_ref[0:8, :] = jnp.zeros((8, D_CONV), F32)

    zbuf_ref[8:8 + tm, :] = z
    zc = (convw_ref[0:1, :] * zbuf_ref[6:6 + tm, :] + convw_ref[1:2, :] * zbuf_ref[7:7 + tm, :]
          + convw_ref[2:3, :] * z)
    zbuf_ref[0:8, :] = zbuf_ref[tm:tm + 8, :]
    yconv_ref[0] = (_rms(u[:, 0:D_CONV] * zc) * gconv_ref[...]).astype(BF16)

    q_ref[0] = (jnp.dot(h, wq_ref[...], preferred_element_type=F32) * (HEAD_DIM ** -0.5)).astype(BF16)
    kv = jnp.dot(h, wkv_ref[...], preferred_element_type=F32).astype(BF16)
    cmp_ref[0] = kv[:, 0:2 * KV_DIM]
    ksel_ref[0] = kv[:, 2 * KV_DIM:3 * KV_DIM]
    vsel_ref[0] = kv[:, 3 * KV_DIM:4 * KV_DIM]
    kwin_ref[0] = kv[:, 4 * KV_DIM:5 * KV_DIM]
    vwin_ref[0] = kv[:, 5 * KV_DIM:6 * KV_DIM]
    gates_ref[0] = jax.nn.sigmoid(jnp.dot(h, wgl_ref[...], preferred_element_type=F32))


def _in_projection(x, mod3, g_pre, w_conv, w_q, w_kv, w_gl, conv_w, g_conv):
    bsz, seq, _ = x.shape
    tm = min(ROW_TILE, seq)
    row = lambda width: pl.BlockSpec((1, tm, width), lambda b, j: (b, j, 0))
    out_shapes = [jax.ShapeDtypeStruct((bsz, seq, w), dt) for w, dt in
                  [(D_CONV, BF16), (D_ATTN, BF16), (2 * KV_DIM, BF16), (KV_DIM, BF16), (KV_DIM, BF16),
                   (KV_DIM, BF16), (KV_DIM, BF16), (LANES, F32)]]
    return pl.pallas_call(
        _inproj_kernel,
        grid=(bsz, seq // tm),
        in_specs=[row(D_MODEL),
                  pl.BlockSpec((1, 1, mod3.shape[2]), lambda b, j: (b, 0, 0)),
                  _const_spec(g_pre.shape), _const_spec(w_conv.shape), _const_spec(w_q.shape),
                  _const_spec(w_kv.shape), _const_spec(w_gl.shape), _const_spec(conv_w.shape),
                  _const_spec(g_conv.shape)],
        out_specs=[row(s.shape[2]) for s in out_shapes],
        out_shape=out_shapes,
        scratch_shapes=[pltpu.VMEM((tm + 8, D_CONV), F32)],
        compiler_params=pltpu.CompilerParams(dimension_semantics=("arbitrary", "arbitrary"),
                                             vmem_limit_bytes=VMEM_LIMIT),
        name="in_projection",
    )(x, mod3, g_pre, w_conv, w_q, w_kv, w_gl, conv_w, g_conv)


def _gelu_tanh(v):
    return v * (0.5 * (1.0 + jnp.tanh(math.sqrt(2.0 / math.pi) * (v + 0.044715 * (v * v * v)))))


def _compress_kernel(x_ref, wa_ref, wb_ref, pa_ref, pb_ref, w2_ref, o_ref):
    x = x_ref[0]
    nch = x.shape[0]
    wa = wa_ref[...]
    wb = wb_ref[...]
    def pos_bias(p_ref, w):
        hi = p_ref[...].astype(BF16)
        lo = (p_ref[...] - hi.astype(F32)).astype(BF16)
        return jnp.dot(hi, w, preferred_element_type=F32) + jnp.dot(lo, w, preferred_element_type=F32)

    bias = pos_bias(pa_ref, wa) + pos_bias(pb_ref, wb)
    first = jnp.dot(x, wa, preferred_element_type=F32)
    second = jnp.dot(x, wb, preferred_element_type=F32)
    hid = _gelu_tanh(first + pltpu.roll(second, nch - 1, 0) + bias[0:1, :])
    o_ref[0] = jnp.dot(hid.astype(BF16), w2_ref[...], preferred_element_type=F32).astype(BF16)


def _compress(cmp_chunks, wa, wb, pa, pb, w2):
    bsz, nch, width = cmp_chunks.shape
    return pl.pallas_call(
        _compress_kernel,
        grid=(bsz,),
        in_specs=[pl.BlockSpec((1, nch, width), lambda b: (b, 0, 0)),
                  _const_spec(wa.shape), _const_spec(wb.shape), _const_spec(pa.shape),
                  _const_spec(pb.shape), _const_spec(w2.shape)],
        out_specs=pl.BlockSpec((1, nch, 2 * KV_DIM), lambda b: (b, 0, 0)),
        out_shape=jax.ShapeDtypeStruct((bsz, nch, 2 * KV_DIM), BF16),
        compiler_params=pltpu.CompilerParams(dimension_semantics=("arbitrary",),
                                             vmem_limit_bytes=VMEM_LIMIT),
        name="compress_kv",
    )(cmp_chunks, wa, wb, pa, pb, w2)


def _group_lhs(qp, g):
    lane = lax.broadcasted_iota(jnp.int32, (1, LANES), 1)
    keep = (lane >= g * HEAD_DIM) & (lane < (g + 1) * HEAD_DIM)
    zero = jnp.zeros((), qp.dtype)
    return jnp.concatenate(
        [jnp.where(keep, qp[:, n * LANES:(n + 1) * LANES], zero) for n in range(HPG)], axis=0)


def _pair_merge(rows_g0, rows_g1, n):
    lane = lax.broadcasted_iota(jnp.int32, (1, LANES), 1)
    return jnp.where(lane < HEAD_DIM, rows_g0[n * Q_TILE:(n + 1) * Q_TILE],
                     rows_g1[n * Q_TILE:(n + 1) * Q_TILE])


def _pair_gate(gates, branch, n):
    lane = lax.broadcasted_iota(jnp.int32, (1, LANES), 1)
    c0 = branch * N_HEADS + n
    c1 = branch * N_HEADS + HPG + n
    g0 = jnp.broadcast_to(gates[:, c0:c0 + 1], (Q_TILE, LANES))
    g1 = jnp.broadcast_to(gates[:, c1:c1 + 1], (Q_TILE, LANES))
    return jnp.where(lane < HEAD_DIM, g0, g1)


def _cmp_select_kernel(q_ref, kvc_ref, gates_ref, ovt_ref, ocmp_ref, selb_ref):
    q0 = pl.program_id(1) * Q_TILE
    qp = q_ref[0]
    kc = kvc_ref[0, :, 0:KV_DIM]
    vc = kvc_ref[0, :, KV_DIM:2 * KV_DIM]
    ncp = kc.shape[0]
    nsb = ovt_ref.shape[0]

    qi = lax.broadcasted_iota(jnp.int32, (Q_TILE, ncp), 0)
    ki = lax.broadcasted_iota(jnp.int32, (Q_TILE, ncp), 1)
    dist_i = (q0 + qi) - (ki * CMP_STRIDE + (CMP_LEN - 1))
    valid = dist_i >= 0
    dist = dist_i.astype(F32)
    t_col = q0 + lax.broadcasted_iota(jnp.int32, (Q_TILE, 1), 0)
    any_valid = (t_col >= CMP_LEN - 1).astype(F32)

    blk = lax.broadcasted_iota(jnp.int32, (nsb, Q_TILE), 0)
    cur = (q0 + lax.broadcasted_iota(jnp.int32, (nsb, Q_TILE), 1)) // SEL_LEN
    forced = (blk == 0) | (blk == cur) | (blk == cur - 1)
    causal = blk <= cur
    blk_f = blk.astype(F32)

    o_rows = []
    for g in range(N_KV):
        s = _dot_nt(_group_lhs(qp, g), kc)
        p_heads = []
        for n in range(HPG):
            sn = jnp.where(valid, s[n * Q_TILE:(n + 1) * Q_TILE] - SLOPES[g][n] * dist, NEG)
            e = jnp.exp(sn - jnp.max(sn, axis=-1, keepdims=True))
            p_heads.append(e * (any_valid / jnp.sum(e, axis=-1, keepdims=True)))
        o_rows.append(jnp.dot(jnp.concatenate(p_heads, axis=0).astype(BF16), vc,
                              preferred_element_type=F32))

        p_sum = p_heads[0] + p_heads[1] + p_heads[2] + p_heads[3]
        p_hi = p_sum.astype(BF16)
        p_lo = (p_sum - p_hi.astype(F32)).astype(BF16)
        imp_t = _dot_nt(ovt_ref[...], p_hi) + _dot_nt(ovt_ref[...], p_lo)
        score = jnp.where(causal, jnp.where(forced, FORCE, imp_t), NEG)

        def pick(_, carry):
            sc, sel = carry
            top = jnp.max(sc, axis=0, keepdims=True)
            first = jnp.min(jnp.where(sc == top, blk_f, float(nsb)), axis=0, keepdims=True)
            hit = blk_f == first
            return jnp.where(hit, -jnp.inf, sc), jnp.where(hit, 1.0, sel)

        _, sel_t = lax.fori_loop(0, min(TOP_N, nsb), pick, (score, jnp.zeros((nsb, Q_TILE), F32)))
        sel_t = jnp.where(causal, sel_t, 0.0)
        selb_ref[0, :, g * nsb:(g + 1) * nsb] = jnp.where(sel_t.T > 0.5, 0.0, NEG).astype(BF16)

    gates = gates_ref[0]
    for n in range(HPG):
        ocmp_ref[0, :, n * LANES:(n + 1) * LANES] = (
            _pair_gate(gates, 0, n) * _pair_merge(o_rows[0], o_rows[1], n))


def _cmp_select(q, kvc, gates, ovt):
    bsz, seq, _ = q.shape
    ncp = kvc.shape[1]
    nsb = ovt.shape[0]
    return pl.pallas_call(
        _cmp_select_kernel,
        grid=(bsz, seq // Q_TILE),
        in_specs=[pl.BlockSpec((1, Q_TILE, D_ATTN), lambda b, i: (b, i, 0)),
                  pl.BlockSpec((1, ncp, 2 * KV_DIM), lambda b, i: (b, 0, 0)),
                  pl.BlockSpec((1, Q_TILE, LANES), lambda b, i: (b, i, 0)),
                  _const_spec(ovt.shape)],
        out_specs=[pl.BlockSpec((1, Q_TILE, D_ATTN), lambda b, i: (b, i, 0)),
                   pl.BlockSpec((1, Q_TILE, N_KV * nsb), lambda b, i: (b, i, 0))],
        out_shape=[jax.ShapeDtypeStruct((bsz, seq, D_ATTN), F32),
                   jax.ShapeDtypeStruct((bsz, seq, N_KV * nsb), BF16)],
        compiler_params=pltpu.CompilerParams(dimension_semantics=("arbitrary", "arbitrary"),
                                             vmem_limit_bytes=VMEM_LIMIT),
        name="cmp_select",
    )(q, kvc, gates, ovt)


def _sel_win_kernel(q_ref, ksel_ref, vsel_ref, kwin_ref, vwin_ref, selb_ref, et_ref, gates_ref, ocmp_ref,
                    gattn_ref, y_ref, m_ref, l_ref, acc_ref):
    q0 = pl.program_id(1) * Q_TILE
    qp = q_ref[0]
    seq = ksel_ref.shape[1]
    nsb = et_ref.shape[1]
    rows = HPG * Q_TILE

    rel_sel = (lax.broadcasted_iota(jnp.int32, (Q_TILE, KV_TILE), 1)
               - lax.broadcasted_iota(jnp.int32, (Q_TILE, KV_TILE), 0))
    rel_win = (lax.broadcasted_iota(jnp.int32, (Q_TILE, WIN_SPAN), 0)
               - lax.broadcasted_iota(jnp.int32, (Q_TILE, WIN_SPAN), 1))
    win_start = pl.multiple_of(jnp.maximum(q0 - WINDOW, 0), Q_TILE)
    n_kv = q0 // KV_TILE + 1

    o_sel = []
    o_win = []
    for g in range(N_KV):
        lhs = _group_lhs(qp, g)

        sel_bias = selb_ref[0, :, g * nsb:(g + 1) * nsb]
        lhs_aug = jnp.concatenate([lhs, jnp.concatenate([sel_bias] * HPG, axis=0)], axis=1)
        m_ref[...] = jnp.full((rows, 1), -3e38, F32)
        l_ref[...] = jnp.zeros((rows, 1), F32)
        acc_ref[...] = jnp.zeros((rows, LANES), F32)

        def kv_step(kt, _):
            k0 = pl.multiple_of(kt * KV_TILE, KV_TILE)
            rhs = jnp.concatenate([ksel_ref[0, pl.ds(k0, KV_TILE), :], et_ref[pl.ds(k0, KV_TILE), :]], axis=1)
            s = _dot_nt(lhs_aug, rhs)
            rel = rel_sel + (k0 - q0)
            relf = rel.astype(F32)
            s = jnp.concatenate(
                [jnp.where(rel <= 0, s[n * Q_TILE:(n + 1) * Q_TILE] + SLOPES[g][n] * relf, NEG)
                 for n in range(HPG)], axis=0)
            m_old = m_ref[...]
            m_new = jnp.maximum(m_old, jnp.max(s, axis=-1, keepdims=True))
            alpha = jnp.exp(m_old - m_new)
            p = jnp.exp(s - m_new)
            l_ref[...] = alpha * l_ref[...] + jnp.sum(p, axis=-1, keepdims=True)
            acc_ref[...] = alpha * acc_ref[...] + jnp.dot(
                p.astype(BF16), vsel_ref[0, pl.ds(k0, KV_TILE), :], preferred_element_type=F32)
            m_ref[...] = m_new
            return 0

        lax.fori_loop(0, n_kv, kv_step, 0)
        o_sel.append(acc_ref[...] / l_ref[...])

        s = _dot_nt(lhs, kwin_ref[0, pl.ds(win_start, WIN_SPAN), :])
        d = rel_win + (q0 - win_start)
        ok = (d >= 0) & (d < WINDOW)
        df = d.astype(F32)
        p_heads = []
        for n in range(HPG):
            sn = jnp.where(ok, s[n * Q_TILE:(n + 1) * Q_TILE] - SLOPES[g][n] * df, NEG)
            e = jnp.exp(sn - jnp.max(sn, axis=-1, keepdims=True))
            p_heads.append(e / jnp.sum(e, axis=-1, keepdims=True))
        o_win.append(jnp.dot(jnp.concatenate(p_heads, axis=0).astype(BF16),
                             vwin_ref[0, pl.ds(win_start, WIN_SPAN), :], preferred_element_type=F32))

    gates = gates_ref[0]
    lane = lax.broadcasted_iota(jnp.int32, (1, LANES), 1)
    low = lane < HEAD_DIM
    for n in range(HPG):
        y = (ocmp_ref[0, :, n * LANES:(n + 1) * LANES]
             + _pair_gate(gates, 1, n) * _pair_merge(o_sel[0], o_sel[1], n)
             + _pair_gate(gates, 2, n) * _pair_merge(o_win[0], o_win[1], n))
        y2 = y * y
        ms_lo = jnp.sum(jnp.where(low, y2, 0.0), axis=-1, keepdims=True) * (1.0 / HEAD_DIM)
        ms_hi = jnp.sum(jnp.where(low, 0.0, y2), axis=-1, keepdims=True) * (1.0 / HEAD_DIM)
        inv = jnp.where(low, lax.rsqrt(ms_lo + EPS), lax.rsqrt(ms_hi + EPS))
        y_ref[0, :, n * LANES:(n + 1) * LANES] = (
            y * inv * gattn_ref[:, n * LANES:(n + 1) * LANES]).astype(BF16)


def _sel_win(q, ksel, vsel, kwin, vwin, selb, et, gates, ocmp, g_attn):
    bsz, seq, _ = q.shape
    nsb = et.shape[1]
    tile = lambda width: pl.BlockSpec((1, Q_TILE, width), lambda b, i: (b, i, 0))
    whole = pl.BlockSpec((1, seq, KV_DIM), lambda b, i: (b, 0, 0))
    rows = HPG * Q_TILE
    return pl.pallas_call(
        _sel_win_kernel,
        grid=(bsz, seq // Q_TILE),
        in_specs=[tile(D_ATTN), whole, whole, whole, whole, tile(N_KV * nsb), _const_spec(et.shape),
                  tile(LANES), tile(D_ATTN), _const_spec(g_attn.shape)],
        out_specs=tile(D_ATTN),
        out_shape=jax.ShapeDtypeStruct((bsz, seq, D_ATTN), BF16),
        scratch_shapes=[pltpu.VMEM((rows, 1), F32), pltpu.VMEM((rows, 1), F32),
                        pltpu.VMEM((rows, LANES), F32)],
        compiler_params=pltpu.CompilerParams(dimension_semantics=("arbitrary", "arbitrary"),
                                             vmem_limit_bytes=VMEM_LIMIT),
        name="sel_win_attn",
    )(q, ksel, vsel, kwin, vwin, selb, et, gates, ocmp, g_attn)


def _out_ffn_kernel(x_ref, yconv_ref, yattn_ref, mod_ref, woc_ref, woa_ref, gpm_ref, gpf_ref,
                    wg_ref, wu_ref, wd_ref, gpo_ref, o_ref, f_ref):
    D = D_MODEL
    ga_m = mod_ref[0, :, 2 * D:3 * D]
    sh_f = mod_ref[0, :, 3 * D:4 * D]
    sc_f = mod_ref[0, :, 4 * D:5 * D]
    ga_f = mod_ref[0, :, 5 * D:6 * D]
    y = (jnp.dot(yconv_ref[0], woc_ref[...], preferred_element_type=F32)
         + jnp.dot(yattn_ref[0], woa_ref[...], preferred_element_type=F32))
    x1 = x_ref[0] + ga_m * (_rms(y) * gpm_ref[...])
    h = (_rms(x1) * gpf_ref[...] * (1.0 + sc_f) + sh_f).astype(BF16)
    for c in range(D_FF // FF_CHUNK):
        cols = slice(c * FF_CHUNK, (c + 1) * FF_CHUNK)
        gate = jnp.dot(h, wg_ref[:, cols], preferred_element_type=F32)
        up = jnp.dot(h, wu_ref[:, cols], preferred_element_type=F32)
        act = (gate * jax.nn.sigmoid(gate) * up).astype(BF16)
        part = jnp.dot(act, wd_ref[cols, :], preferred_element_type=F32)
        if c == 0:
            f_ref[...] = part
        else:
            f_ref[...] += part
    o_ref[0] = x1 + ga_f * (_rms(f_ref[...]) * gpo_ref[...])


def _out_ffn(x, yconv, yattn, mod3, wo_c, wo_a, g_pm, g_pf, wg, wu, wd, g_po):
    bsz, seq, _ = x.shape
    tm = min(ROW_TILE, seq)
    row = lambda width: pl.BlockSpec((1, tm, width), lambda b, j: (b, j, 0))
    return pl.pallas_call(
        _out_ffn_kernel,
        grid=(bsz, seq // tm),
        in_specs=[row(D_MODEL), row(D_CONV), row(D_ATTN),
                  pl.BlockSpec((1, 1, mod3.shape[2]), lambda b, j: (b, 0, 0)),
                  _const_spec(wo_c.shape), _const_spec(wo_a.shape), _const_spec(g_pm.shape),
                  _const_spec(g_pf.shape), _const_spec(wg.shape), _const_spec(wu.shape),
                  _const_spec(wd.shape), _const_spec(g_po.shape)],
        out_specs=row(D_MODEL),
        out_shape=jax.ShapeDtypeStruct((bsz, seq, D_MODEL), F32),
        scratch_shapes=[pltpu.VMEM((tm, D_MODEL), F32)],
        compiler_params=pltpu.CompilerParams(dimension_semantics=("arbitrary", "arbitrary"),
                                             vmem_limit_bytes=VMEM_LIMIT),
        name="out_ffn",
    )(x, yconv, yattn, mod3, wo_c, wo_a, g_pm, g_pf, wg, wu, wd, g_po)


def _pair_perm():
    idx = []
    for n in range(HPG):
        for g in range(N_KV):
            base = g * HPG * HEAD_DIM + n * HEAD_DIM
            idx.extend(range(base, base + HEAD_DIM))
    return jnp.array(idx, jnp.int32)


def _gate_perm():
    return jnp.array([h * 3 + j for j in range(3) for h in range(N_HEADS)], jnp.int32)


def _expand_w1(w1_half_k, w1_half_v):
    eye_g = jnp.eye(N_KV, dtype=F32)
    blocks = []
    for kv, w in enumerate((w1_half_k, w1_half_v)):
        e = jnp.einsum("ldh,gk->lgdkh", w, eye_g).reshape(CMP_STRIDE, KV_DIM, N_KV * CMP_HIDDEN)
        pad = jnp.zeros_like(e)
        blocks.append(jnp.concatenate([e, pad] if kv == 0 else [pad, e], axis=2))
    return jnp.concatenate(blocks, axis=1).reshape(CMP_STRIDE * 2 * KV_DIM, 2 * N_KV * CMP_HIDDEN)


def _expand_pos(pos_half):
    rowv = jnp.tile(pos_half[:, None, None, :], (1, 2, N_KV, 1)).reshape(1, -1)
    return jnp.tile(rowv, (8, 1))


def _expand_w2(w2_k, w2_v):
    eye_g = jnp.eye(N_KV, dtype=F32)
    out = []
    for kv, w in enumerate((w2_k, w2_v)):
        e = jnp.einsum("hd,gk->ghkd", w, eye_g).reshape(N_KV * CMP_HIDDEN, KV_DIM)
        pad = jnp.zeros_like(e)
        out.append(jnp.concatenate([e, pad] if kv == 0 else [pad, e], axis=1))
    return jnp.concatenate(out, axis=0)


def kernel(x, c, w_mod, b_mod, g_pre_mix, w_in, conv_w, pos_cmp, w_ck1, w_ck2, w_cv1, w_cv2, g_conv_out,
           g_attn_out, w_out, g_post_mix, g_pre_ffn, w_gate, w_up, w_down, g_post_ffn):
    bsz, seq, _ = x.shape
    assert seq % KV_TILE == 0 and seq >= WIN_SPAN and w_mod.shape[0] == 1
    nsb = seq // SEL_LEN
    nch = seq // CMP_STRIDE
    l = 0
    perm = _pair_perm()

    o_q = 3 * D_CONV
    o_kv = o_q + D_ATTN
    o_gl = o_kv + 6 * KV_DIM
    w_in_l = w_in[l]
    w_conv = w_in_l[:, :o_q].astype(BF16)
    w_q = w_in_l[:, o_q:o_kv][:, perm].astype(BF16)
    w_kv = w_in_l[:, o_kv:o_gl].astype(BF16)
    w_gl = jnp.pad(w_in_l[:, o_gl:][:, _gate_perm()], ((0, 0), (0, LANES - 3 * N_HEADS))).astype(BF16)

    mod3 = _modulation(c, w_mod[l], b_mod[l]).reshape(bsz, 1, 6 * D_MODEL)
    yconv, q, cmp_kv, ksel, vsel, kwin, vwin, gates = _in_projection(
        x, mod3, g_pre_mix[l][None], w_conv, w_q, w_kv, w_gl, conv_w[l], g_conv_out[l][None])

    kvc = _compress(
        cmp_kv.reshape(bsz, nch, CMP_STRIDE * 2 * KV_DIM),
        _expand_w1(w_ck1[l][:CMP_STRIDE], w_cv1[l][:CMP_STRIDE]).astype(BF16),
        _expand_w1(w_ck1[l][CMP_STRIDE:], w_cv1[l][CMP_STRIDE:]).astype(BF16),
        _expand_pos(pos_cmp[l][:CMP_STRIDE]), _expand_pos(pos_cmp[l][CMP_STRIDE:]),
        _expand_w2(w_ck2[l], w_cv2[l]).astype(BF16))

    ck = jnp.arange(nch)[None, :] * CMP_STRIDE
    sj = jnp.arange(nsb)[:, None] * SEL_LEN
    ovt = ((ck < sj + SEL_LEN) & (ck + CMP_LEN > sj) & (ck < (nch - 1) * CMP_STRIDE)).astype(BF16)
    ocmp, selb = _cmp_select(q, kvc, gates, ovt)

    et = (jnp.arange(seq)[:, None] // SEL_LEN == jnp.arange(nsb)[None, :]).astype(BF16)
    yattn = _sel_win(q, ksel, vsel, kwin, vwin, selb, et, gates, ocmp, g_attn_out[l][perm][None])

    w_out_l = w_out[l]
    return _out_ffn(x, yconv, yattn, mod3, w_out_l[:D_CONV].astype(BF16), w_out_l[D_CONV:][perm].astype(BF16),
                    g_post_mix[l][None], g_pre_ffn[l][None], w_gate[l].astype(BF16), w_up[l].astype(BF16),
                    w_down[l].astype(BF16), g_post_ffn[l][None])
```

```python
import functools
import math

import jax
import jax.numpy as jnp
from jax import lax
from jax.experimental import pallas as pl
from jax.experimental.pallas import tpu as pltpu

F32 = jnp.float32
BF16 = jnp.bfloat16

D_MODEL = 1024
D_CONV = 512
CONV_WIDTH = 3
N_HEADS = 8
HEAD_DIM = 64
N_KV = 2
HPG = N_HEADS // N_KV
D_ATTN = N_HEADS * HEAD_DIM
KV_DIM = N_KV * HEAD_DIM
CMP_LEN = 32
CMP_STRIDE = 16
CMP_HIDDEN = 2 * HEAD_DIM
SEL_LEN = 64
TOP_N = 16
WINDOW = 512
D_FF = 2816
NEG = -1e30
FORCE = 1e9
EPS = 1e-6

LANES = 128
Q_TILE = 128
KV_TILE = 512
FLAG_BITS = 16
WIN_SPAN = WINDOW + Q_TILE
ROW_TILE = 512
FF_CHUNK = 256
VMEM_LIMIT = 56 * 1024 * 1024

SLOPES = [[2.0 ** (-(g * HPG + n + 1)) for n in range(HPG)] for g in range(N_KV)]


def _const_spec(shape):
    nd = len(shape)
    return pl.BlockSpec(shape, lambda *_: (0,) * nd, pipeline_mode=pl.Buffered(1))


def _rms(v):
    return v * lax.rsqrt(jnp.mean(v * v, axis=-1, keepdims=True) + EPS)


def _dot_nt(a, b):
    return lax.dot_general(a, b, (((1,), (1,)), ((), ())), preferred_element_type=F32)


def _mod_kernel(c_ref, w_ref, b_ref, o_ref):
    c = c_ref[...]
    a = c * jax.nn.sigmoid(c)
    o_ref[...] = jnp.dot(a, w_ref[...], preferred_element_type=F32,
                         precision=lax.Precision.HIGHEST) + b_ref[...]


def _modulation(c, w_mod, b_mod):
    bsz = c.shape[0]
    n_out = w_mod.shape[1]
    blk = D_MODEL
    return pl.pallas_call(
        _mod_kernel,
        grid=(n_out // blk,),
        in_specs=[pl.BlockSpec((bsz, D_MODEL), lambda j: (0, 0)),
                  pl.BlockSpec((D_MODEL, blk), lambda j: (0, j)),
                  pl.BlockSpec((1, blk), lambda j: (0, j))],
        out_specs=pl.BlockSpec((bsz, blk), lambda j: (0, j)),
        out_shape=jax.ShapeDtypeStruct((bsz, n_out), F32),
        compiler_params=pltpu.CompilerParams(dimension_semantics=("arbitrary",),
                                             vmem_limit_bytes=VMEM_LIMIT),
        name="adaln_mod",
    )(c, w_mod, b_mod.reshape(1, n_out))


def _aug_keys(k, feat):
    low = lax.broadcasted_iota(jnp.int32, (1, LANES), 1) < HEAD_DIM
    return jnp.concatenate([jnp.where(low, k, feat), jnp.where(low, feat, k)], axis=1).astype(BF16)


def _inproj_kernel(x_ref, mod_ref, gpre_ref, wconv_ref, wq_ref, wkv_ref, wgl_ref, convw_ref, gconv_ref,
                   yconv_ref, q_ref, cmp_ref, ksel_ref, vsel_ref, kwin_ref, vwin_ref, gates_ref,
                   zbuf_ref):
    tm = x_ref.shape[1]
    x = x_ref[0]
    sh = mod_ref[0, :, 0:D_MODEL]
    sc = mod_ref[0, :, D_MODEL:2 * D_MODEL]
    h = (_rms(x) * gpre_ref[...] * (1.0 + sc) + sh).astype(BF16)

    u = jnp.dot(h, wconv_ref[...], preferred_element_type=F32)
    z = u[:, D_CONV:2 * D_CONV] * u[:, 2 * D_CONV:]

    @pl.when(pl.program_id(1) == 0)
    def _():
        zbuf_ref[0:8, :] = jnp.zeros((8, D_CONV), F32)

    zbuf_ref[8:8 + tm, :] = z
    zc = (convw_ref[0:1, :] * zbuf_ref[6:6 + tm, :] + convw_ref[1:2, :] * zbuf_ref[7:7 + tm, :]
          + convw_ref[2:3, :] * z)
    zbuf_ref[0:8, :] = zbuf_ref[tm:tm + 8, :]
    yconv_ref[0] = (_rms(u[:, 0:D_CONV] * zc) * gconv_ref[...]).astype(BF16)

    q_ref[0] = (jnp.dot(h, wq_ref[...], preferred_element_type=F32) * (HEAD_DIM ** -0.5)).astype(BF16)
    kv = jnp.dot(h, wkv_ref[...], preferred_element_type=F32)
    cmp_ref[0] = kv[:, 0:2 * KV_DIM].astype(BF16)
    vsel_ref[0] = kv[:, 3 * KV_DIM:4 * KV_DIM].astype(BF16)
    vwin_ref[0] = kv[:, 5 * KV_DIM:6 * KV_DIM].astype(BF16)
    kpos = pl.program_id(1) * tm + lax.broadcasted_iota(jnp.int32, (tm, LANES), 0)
    fl = lax.broadcasted_iota(jnp.int32, (tm, LANES), 1) % HEAD_DIM
    feat = jnp.where(fl == 0, kpos // SEL_LEN,
                     jnp.where(fl == 1, kpos % SEL_LEN, jnp.where(fl < 4, 1, 0))).astype(F32)
    ksel_ref[0] = _aug_keys(kv[:, 2 * KV_DIM:3 * KV_DIM], feat)
    kwin_ref[0] = _aug_keys(kv[:, 4 * KV_DIM:5 * KV_DIM], feat)
    gates_ref[0] = jax.nn.sigmoid(jnp.dot(h, wgl_ref[...], preferred_element_type=F32))


def _in_projection(x, mod3, g_pre, w_conv, w_q, w_kv, w_gl, conv_w, g_conv):
    bsz, seq, _ = x.shape
    tm = min(ROW_TILE, seq)
    row = lambda width: pl.BlockSpec((1, tm, width), lambda b, j: (b, j, 0))
    out_shapes = [jax.ShapeDtypeStruct((bsz, seq, w), dt) for w, dt in
                  [(D_CONV, BF16), (D_ATTN, BF16), (2 * KV_DIM, BF16), (2 * LANES, BF16), (KV_DIM, BF16),
                   (2 * LANES, BF16), (KV_DIM, BF16), (LANES, F32)]]
    return pl.pallas_call(
        _inproj_kernel,
        grid=(bsz, seq // tm),
        in_specs=[row(D_MODEL),
                  pl.BlockSpec((1, 1, mod3.shape[2]), lambda b, j: (b, 0, 0)),
                  _const_spec(g_pre.shape), _const_spec(w_conv.shape), _const_spec(w_q.shape),
                  _const_spec(w_kv.shape), _const_spec(w_gl.shape), _const_spec(conv_w.shape),
                  _const_spec(g_conv.shape)],
        out_specs=[row(s.shape[2]) for s in out_shapes],
        out_shape=out_shapes,
        scratch_shapes=[pltpu.VMEM((tm + 8, D_CONV), F32)],
        compiler_params=pltpu.CompilerParams(dimension_semantics=("arbitrary", "arbitrary"),
                                             vmem_limit_bytes=VMEM_LIMIT),
        name="in_projection",
    )(x, mod3, g_pre, w_conv, w_q, w_kv, w_gl, conv_w, g_conv)


def _gelu_tanh(v):
    return v * (0.5 * (1.0 + jnp.tanh(math.sqrt(2.0 / math.pi) * (v + 0.044715 * (v * v * v)))))


def _compress_kernel(x_ref, wa_ref, wb_ref, pa_ref, pb_ref, w2_ref, o_ref):
    x = x_ref[0]
    nch = x.shape[0]
    wa = wa_ref[...]
    wb = wb_ref[...]
    def pos_bias(p_ref, w):
        hi = p_ref[...].astype(BF16)
        lo = (p_ref[...] - hi.astype(F32)).astype(BF16)
        return jnp.dot(hi, w, preferred_element_type=F32) + jnp.dot(lo, w, preferred_element_type=F32)

    bias = pos_bias(pa_ref, wa) + pos_bias(pb_ref, wb)
    first = jnp.dot(x, wa, preferred_element_type=F32)
    second = jnp.dot(x, wb, preferred_element_type=F32)
    hid = _gelu_tanh(first + pltpu.roll(second, nch - 1, 0) + bias[0:1, :])
    o_ref[0] = jnp.dot(hid.astype(BF16), w2_ref[...], preferred_element_type=F32).astype(BF16)


def _compress(cmp_chunks, wa, wb, pa, pb, w2):
    bsz, nch, width = cmp_chunks.shape
    return pl.pallas_call(
        _compress_kernel,
        grid=(bsz,),
        in_specs=[pl.BlockSpec((1, nch, width), lambda b: (b, 0, 0)),
                  _const_spec(wa.shape), _const_spec(wb.shape), _const_spec(pa.shape),
                  _const_spec(pb.shape), _const_spec(w2.shape)],
        out_specs=pl.BlockSpec((1, nch, 2 * KV_DIM), lambda b: (b, 0, 0)),
        out_shape=jax.ShapeDtypeStruct((bsz, nch, 2 * KV_DIM), BF16),
        compiler_params=pltpu.CompilerParams(dimension_semantics=("arbitrary",),
                                             vmem_limit_bytes=VMEM_LIMIT),
        name="compress_kv",
    )(cmp_chunks, wa, wb, pa, pb, w2)


def _group_lhs(qp, g):
    lane = lax.broadcasted_iota(jnp.int32, (1, LANES), 1)
    keep = (lane >= g * HEAD_DIM) & (lane < (g + 1) * HEAD_DIM)
    zero = jnp.zeros((), qp.dtype)
    return jnp.concatenate(
        [jnp.where(keep, qp[:, n * LANES:(n + 1) * LANES], zero) for n in range(HPG)], axis=0)


def _pair_merge(rows_g0, rows_g1, n):
    lane = lax.broadcasted_iota(jnp.int32, (1, LANES), 1)
    return jnp.where(lane < HEAD_DIM, rows_g0[n * Q_TILE:(n + 1) * Q_TILE],
                     rows_g1[n * Q_TILE:(n + 1) * Q_TILE])


def _pair_gate(gates, branch, n):
    lane = lax.broadcasted_iota(jnp.int32, (1, LANES), 1)
    c0 = branch * N_HEADS + n
    c1 = branch * N_HEADS + HPG + n
    g0 = jnp.broadcast_to(gates[:, c0:c0 + 1], (Q_TILE, LANES))
    g1 = jnp.broadcast_to(gates[:, c1:c1 + 1], (Q_TILE, LANES))
    return jnp.where(lane < HEAD_DIM, g0, g1)


def _cmp_select_kernel(q_ref, kvc_ref, gates_ref, ovt_ref, grp_ref, ocmp_ref, selb_ref, flags_ref):
    q0 = pl.program_id(1) * Q_TILE
    flags_ref[0, 0] = jnp.zeros((8, LANES), jnp.int32)
    qp = q_ref[0]
    kc = kvc_ref[0, :, 0:KV_DIM]
    vc = kvc_ref[0, :, KV_DIM:2 * KV_DIM]
    ncp = kc.shape[0]
    nsb = ovt_ref.shape[0]

    qi = lax.broadcasted_iota(jnp.int32, (Q_TILE, ncp), 0)
    ki = lax.broadcasted_iota(jnp.int32, (Q_TILE, ncp), 1)
    dist_i = (q0 + qi) - (ki * CMP_STRIDE + (CMP_LEN - 1))
    valid = dist_i >= 0
    dist = dist_i.astype(F32)
    t_col = q0 + lax.broadcasted_iota(jnp.int32, (Q_TILE, 1), 0)
    any_valid = (t_col >= CMP_LEN - 1).astype(F32)

    blk = lax.broadcasted_iota(jnp.int32, (nsb, Q_TILE), 0)
    cur = (q0 + lax.broadcasted_iota(jnp.int32, (nsb, Q_TILE), 1)) // SEL_LEN
    forced = (blk == 0) | (blk == cur) | (blk == cur - 1)
    causal = blk <= cur
    blk_f = blk.astype(F32)

    o_rows = []
    for g in range(N_KV):
        s = _dot_nt(_group_lhs(qp, g), kc)
        p_heads = []
        for n in range(HPG):
            sn = jnp.where(valid, s[n * Q_TILE:(n + 1) * Q_TILE] - SLOPES[g][n] * dist, NEG)
            e = jnp.exp(sn - jnp.max(sn, axis=-1, keepdims=True))
            p_heads.append(e * (any_valid / jnp.sum(e, axis=-1, keepdims=True)))
        o_rows.append(jnp.dot(jnp.concatenate(p_heads, axis=0).astype(BF16), vc,
                              preferred_element_type=F32))

        p_sum = p_heads[0] + p_heads[1] + p_heads[2] + p_heads[3]
        p_hi = p_sum.astype(BF16)
        p_lo = (p_sum - p_hi.astype(F32)).astype(BF16)
        imp_t = _dot_nt(ovt_ref[...], p_hi) + _dot_nt(ovt_ref[...], p_lo)
        score = jnp.where(causal, jnp.where(forced, FORCE, imp_t), NEG)

        def pick(_, carry):
            sc, sel = carry
            top = jnp.max(sc, axis=0, keepdims=True)
            first = jnp.min(jnp.where(sc == top, blk_f, float(nsb)), axis=0, keepdims=True)
            hit = blk_f == first
            return jnp.where(hit, -jnp.inf, sc), jnp.where(hit, 1.0, sel)

        _, sel_t = lax.fori_loop(0, min(TOP_N, nsb), pick, (score, jnp.zeros((nsb, Q_TILE), F32)))
        sel = jnp.where(blk < q0 // SEL_LEN, sel_t, 0.0).T
        selb_ref[0, :, g * nsb:(g + 1) * nsb] = jnp.where(sel > 0.5, 0.0, NEG).astype(BF16)

        col_any = jnp.broadcast_to(jnp.max(sel, axis=0, keepdims=True), (8, nsb)).astype(BF16)
        hit = jnp.dot(col_any, grp_ref[...], preferred_element_type=F32) > 0.0
        lane8 = lax.broadcasted_iota(jnp.int32, (8, LANES), 1)
        weighted = jnp.where(hit, jnp.left_shift(1, lane8 % FLAG_BITS).astype(F32), 0.0)
        words = jnp.zeros((8, LANES), F32)
        for w in range(LANES // FLAG_BITS):
            word = jnp.sum(jnp.where(lane8 // FLAG_BITS == w, weighted, 0.0), axis=1, keepdims=True)
            words = jnp.where(lane8 == w, word, words)
        flags_ref[0, 0, g:g + 1, :] = words[0:1].astype(jnp.int32)

    gates = gates_ref[0]
    for n in range(HPG):
        ocmp_ref[0, :, n * LANES:(n + 1) * LANES] = (
            _pair_gate(gates, 0, n) * _pair_merge(o_rows[0], o_rows[1], n))


def _cmp_select(q, kvc, gates, ovt, grp):
    bsz, seq, _ = q.shape
    ncp = kvc.shape[1]
    nsb = ovt.shape[0]
    nq = seq // Q_TILE
    return pl.pallas_call(
        _cmp_select_kernel,
        grid=(bsz, nq),
        in_specs=[pl.BlockSpec((1, Q_TILE, D_ATTN), lambda b, i: (b, i, 0)),
                  pl.BlockSpec((1, ncp, 2 * KV_DIM), lambda b, i: (b, 0, 0)),
                  pl.BlockSpec((1, Q_TILE, LANES), lambda b, i: (b, i, 0)),
                  _const_spec(ovt.shape), _const_spec(grp.shape)],
        out_specs=[pl.BlockSpec((1, Q_TILE, D_ATTN), lambda b, i: (b, i, 0)),
                   pl.BlockSpec((1, Q_TILE, N_KV * nsb), lambda b, i: (b, i, 0)),
                   pl.BlockSpec((1, 1, 8, LANES), lambda b, i: (b, i, 0, 0))],
        out_shape=[jax.ShapeDtypeStruct((bsz, seq, D_ATTN), F32),
                   jax.ShapeDtypeStruct((bsz, seq, N_KV * nsb), BF16),
                   jax.ShapeDtypeStruct((bsz, nq, 8, LANES), jnp.int32)],
        compiler_params=pltpu.CompilerParams(dimension_semantics=("arbitrary", "arbitrary"),
                                             vmem_limit_bytes=VMEM_LIMIT),
        name="cmp_select",
    )(q, kvc, gates, ovt, grp)


def _softmax_pv(s, v_ext):
    p = jnp.exp(s - jnp.max(s, axis=-1, keepdims=True)).astype(BF16)
    return jnp.dot(p, v_ext, preferred_element_type=F32)


def _sel_win_kernel(flags_ref, q_ref, ksel_ref, vsel_ref, kwin_ref, vwin_ref, selb_ref, et_ref, gates_ref,
                    ocmp_ref, gattn_ref, y_ref, m_ref, acc_ref):
    tile_id = pl.program_id(0) * pl.num_programs(1) + pl.program_id(1)
    q0 = pl.program_id(1) * Q_TILE
    qp = q_ref[0]
    nsb = et_ref.shape[1]
    rows = HPG * Q_TILE
    n_words = pl.cdiv(ksel_ref.shape[1] // KV_TILE, FLAG_BITS)

    lane = lax.broadcasted_iota(jnp.int32, (1, LANES), 1)
    qi = lax.broadcasted_iota(jnp.int32, (Q_TILE, Q_TILE), 0)
    ki = lax.broadcasted_iota(jnp.int32, (Q_TILE, Q_TILE), 1)
    tri = jnp.where(ki <= qi, 0.0, NEG)
    tri4 = jnp.concatenate([tri] * HPG, axis=0)
    win_start = pl.multiple_of(jnp.maximum(q0 - WINDOW, 0), Q_TILE)
    dw = ((q0 - win_start) + lax.broadcasted_iota(jnp.int32, (Q_TILE, WIN_SPAN), 0)
          - lax.broadcasted_iota(jnp.int32, (Q_TILE, WIN_SPAN), 1))
    wbias = jnp.where((dw >= 0) & (dw < WINDOW), 0.0, NEG)
    wbias4 = jnp.concatenate([wbias] * HPG, axis=0)
    ones_q = jnp.ones((Q_TILE, LANES), BF16)
    ones_kv = jnp.ones((KV_TILE, LANES), BF16)
    ones_w = jnp.ones((WIN_SPAN, LANES), BF16)

    t = q0 + lax.broadcasted_iota(jnp.int32, (Q_TILE, LANES), 0)
    t_hi = (t // SEL_LEN).astype(F32)
    t_lo = (t % SEL_LEN).astype(F32)
    fl = lax.broadcasted_iota(jnp.int32, (Q_TILE, LANES), 1) % HEAD_DIM
    n_past = (q0 + KV_TILE - 1) // KV_TILE

    o_sel = []
    o_win = []
    for g in range(N_KV):
        keep = (lane >= g * HEAD_DIM) & (lane < (g + 1) * HEAD_DIM)
        heads = []
        for n in range(HPG):
            sl = SLOPES[g][n]
            feat = jnp.where(fl == 0, sl * SEL_LEN,
                             jnp.where(fl == 1, sl,
                                       jnp.where(fl == 2, -(sl * SEL_LEN) * t_hi,
                                                 jnp.where(fl == 3, -sl * t_lo, 0.0))))
            heads.append(jnp.where(keep, qp[:, n * LANES:(n + 1) * LANES], feat.astype(BF16)))
        lhs = jnp.concatenate(heads, axis=0)
        ks = slice(g * LANES, (g + 1) * LANES)

        s = _dot_nt(lhs, ksel_ref[0, pl.ds(q0, Q_TILE), ks]) + tri4
        m0 = jnp.max(s, axis=-1, keepdims=True)
        p = jnp.exp(s - m0).astype(BF16)
        acc_ref[...] = jnp.dot(p, jnp.concatenate([vsel_ref[0, pl.ds(q0, Q_TILE), :], ones_q], axis=1),
                               preferred_element_type=F32)
        m_ref[...] = jnp.broadcast_to(m0, (rows, LANES))

        sel_bias = selb_ref[0, :, g * nsb:(g + 1) * nsb]
        lhs_aug = jnp.concatenate([lhs, jnp.concatenate([sel_bias] * HPG, axis=0)], axis=1)

        def kv_step(kt, carry):
            word = flags_ref[(tile_id * N_KV + g) * n_words + kt // FLAG_BITS]

            @pl.when(((word >> (kt % FLAG_BITS)) & 1) == 1)
            def _():
                k0 = pl.multiple_of(kt * KV_TILE, KV_TILE)
                rhs = jnp.concatenate([ksel_ref[0, pl.ds(k0, KV_TILE), ks], et_ref[pl.ds(k0, KV_TILE), :]],
                                      axis=1)
                s = _dot_nt(lhs_aug, rhs)
                m_old = m_ref[...]
                m_new = jnp.maximum(m_old, jnp.max(s, axis=-1, keepdims=True))
                alpha = jnp.exp(m_old - m_new)
                p = jnp.exp(s - jnp.concatenate([m_new] * (KV_TILE // LANES), axis=1)).astype(BF16)
                pv = jnp.dot(p, jnp.concatenate([vsel_ref[0, pl.ds(k0, KV_TILE), :], ones_kv], axis=1),
                             preferred_element_type=F32)
                acc_ref[...] = jnp.concatenate([alpha, alpha], axis=1) * acc_ref[...] + pv
                m_ref[...] = m_new

            return carry

        lax.fori_loop(0, n_past, kv_step, 0)
        acc = acc_ref[...]
        o_sel.append(acc[:, 0:LANES] / acc[:, LANES:2 * LANES])

        s = _dot_nt(lhs, kwin_ref[0, pl.ds(win_start, WIN_SPAN), ks]) + wbias4
        ow = _softmax_pv(s, jnp.concatenate([vwin_ref[0, pl.ds(win_start, WIN_SPAN), :], ones_w], axis=1))
        o_win.append(ow[:, 0:LANES] / ow[:, LANES:2 * LANES])

    gates = gates_ref[0]
    lane = lax.broadcasted_iota(jnp.int32, (1, LANES), 1)
    low = lane < HEAD_DIM
    for n in range(HPG):
        y = (ocmp_ref[0, :, n * LANES:(n + 1) * LANES]
             + _pair_gate(gates, 1, n) * _pair_merge(o_sel[0], o_sel[1], n)
             + _pair_gate(gates, 2, n) * _pair_merge(o_win[0], o_win[1], n))
        y2 = y * y
        ms_lo = jnp.sum(jnp.where(low, y2, 0.0), axis=-1, keepdims=True) * (1.0 / HEAD_DIM)
        ms_hi = jnp.sum(jnp.where(low, 0.0, y2), axis=-1, keepdims=True) * (1.0 / HEAD_DIM)
        inv = jnp.where(low, lax.rsqrt(ms_lo + EPS), lax.rsqrt(ms_hi + EPS))
        y_ref[0, :, n * LANES:(n + 1) * LANES] = (
            y * inv * gattn_ref[:, n * LANES:(n + 1) * LANES]).astype(BF16)


def _sel_win(flags, q, ksel, vsel, kwin, vwin, selb, et, gates, ocmp, g_attn):
    bsz, seq, _ = q.shape
    nsb = et.shape[1]
    tile = lambda width: pl.BlockSpec((1, Q_TILE, width), lambda b, i, *_: (b, i, 0))
    whole = lambda width: pl.BlockSpec((1, seq, width), lambda b, i, *_: (b, 0, 0))
    rows = HPG * Q_TILE
    return pl.pallas_call(
        _sel_win_kernel,
        grid_spec=pltpu.PrefetchScalarGridSpec(
            num_scalar_prefetch=1,
            grid=(bsz, seq // Q_TILE),
            in_specs=[tile(D_ATTN), whole(2 * LANES), whole(KV_DIM), whole(2 * LANES), whole(KV_DIM),
                      tile(N_KV * nsb), _const_spec(et.shape), tile(LANES), tile(D_ATTN),
                      _const_spec(g_attn.shape)],
            out_specs=tile(D_ATTN),
            scratch_shapes=[pltpu.VMEM((rows, LANES), F32), pltpu.VMEM((rows, 2 * LANES), F32)]),
        out_shape=jax.ShapeDtypeStruct((bsz, seq, D_ATTN), BF16),
        compiler_params=pltpu.CompilerParams(dimension_semantics=("arbitrary", "arbitrary"),
                                             vmem_limit_bytes=VMEM_LIMIT),
        name="sel_win_attn",
    )(flags, q, ksel, vsel, kwin, vwin, selb, et, gates, ocmp, g_attn)


def _out_ffn_kernel(x_ref, yconv_ref, yattn_ref, mod_ref, woc_ref, woa_ref, gpm_ref, gpf_ref,
                    wg_ref, wu_ref, wd_ref, gpo_ref, o_ref, f_ref):
    D = D_MODEL
    ga_m = mod_ref[0, :, 2 * D:3 * D]
    sh_f = mod_ref[0, :, 3 * D:4 * D]
    sc_f = mod_ref[0, :, 4 * D:5 * D]
    ga_f = mod_ref[0, :, 5 * D:6 * D]
    y = (jnp.dot(yconv_ref[0], woc_ref[...], preferred_element_type=F32)
         + jnp.dot(yattn_ref[0], woa_ref[...], preferred_element_type=F32))
    x1 = x_ref[0] + ga_m * (_rms(y) * gpm_ref[...])
    h = (_rms(x1) * gpf_ref[...] * (1.0 + sc_f) + sh_f).astype(BF16)
    for c in range(D_FF // FF_CHUNK):
        cols = slice(c * FF_CHUNK, (c + 1) * FF_CHUNK)
        gate = jnp.dot(h, wg_ref[:, cols], preferred_element_type=F32)
        up = jnp.dot(h, wu_ref[:, cols], preferred_element_type=F32)
        act = (gate * jax.nn.sigmoid(gate) * up).astype(BF16)
        part = jnp.dot(act, wd_ref[cols, :], preferred_element_type=F32)
        if c == 0:
            f_ref[...] = part
        else:
            f_ref[...] += part
    o_ref[0] = x1 + ga_f * (_rms(f_ref[...]) * gpo_ref[...])


def _out_ffn(x, yconv, yattn, mod3, wo_c, wo_a, g_pm, g_pf, wg, wu, wd, g_po):
    bsz, seq, _ = x.shape
    tm = min(ROW_TILE, seq)
    row = lambda width: pl.BlockSpec((1, tm, width), lambda b, j: (b, j, 0))
    return pl.pallas_call(
        _out_ffn_kernel,
        grid=(bsz, seq // tm),
        in_specs=[row(D_MODEL), row(D_CONV), row(D_ATTN),
                  pl.BlockSpec((1, 1, mod3.shape[2]), lambda b, j: (b, 0, 0)),
                  _const_spec(wo_c.shape), _const_spec(wo_a.shape), _const_spec(g_pm.shape),
                  _const_spec(g_pf.shape), _const_spec(wg.shape), _const_spec(wu.shape),
                  _const_spec(wd.shape), _const_spec(g_po.shape)],
        out_specs=row(D_MODEL),
        out_shape=jax.ShapeDtypeStruct((bsz, seq, D_MODEL), F32),
        scratch_shapes=[pltpu.VMEM((tm, D_MODEL), F32)],
        compiler_params=pltpu.CompilerParams(dimension_semantics=("arbitrary", "arbitrary"),
                                             vmem_limit_bytes=VMEM_LIMIT),
        name="out_ffn",
    )(x, yconv, yattn, mod3, wo_c, wo_a, g_pm, g_pf, wg, wu, wd, g_po)


def _pair_perm():
    idx = []
    for n in range(HPG):
        for g in range(N_KV):
            base = g * HPG * HEAD_DIM + n * HEAD_DIM
            idx.extend(range(base, base + HEAD_DIM))
    return jnp.array(idx, jnp.int32)


def _gate_perm():
    return jnp.array([h * 3 + j for j in range(3) for h in range(N_HEADS)], jnp.int32)


def _expand_w1(w1_half_k, w1_half_v):
    eye_g = jnp.eye(N_KV, dtype=F32)
    blocks = []
    for kv, w in enumerate((w1_half_k, w1_half_v)):
        e = jnp.einsum("ldh,gk->lgdkh", w, eye_g).reshape(CMP_STRIDE, KV_DIM, N_KV * CMP_HIDDEN)
        pad = jnp.zeros_like(e)
        blocks.append(jnp.concatenate([e, pad] if kv == 0 else [pad, e], axis=2))
    return jnp.concatenate(blocks, axis=1).reshape(CMP_STRIDE * 2 * KV_DIM, 2 * N_KV * CMP_HIDDEN)


def _expand_pos(pos_half):
    rowv = jnp.tile(pos_half[:, None, None, :], (1, 2, N_KV, 1)).reshape(1, -1)
    return jnp.tile(rowv, (8, 1))


def _expand_w2(w2_k, w2_v):
    eye_g = jnp.eye(N_KV, dtype=F32)
    out = []
    for kv, w in enumerate((w2_k, w2_v)):
        e = jnp.einsum("hd,gk->ghkd", w, eye_g).reshape(N_KV * CMP_HIDDEN, KV_DIM)
        pad = jnp.zeros_like(e)
        out.append(jnp.concatenate([e, pad] if kv == 0 else [pad, e], axis=1))
    return jnp.concatenate(out, axis=0)


def kernel(x, c, w_mod, b_mod, g_pre_mix, w_in, conv_w, pos_cmp, w_ck1, w_ck2, w_cv1, w_cv2, g_conv_out,
           g_attn_out, w_out, g_post_mix, g_pre_ffn, w_gate, w_up, w_down, g_post_ffn):
    bsz, seq, _ = x.shape
    assert seq % KV_TILE == 0 and seq >= WIN_SPAN and w_mod.shape[0] == 1
    nsb = seq // SEL_LEN
    nch = seq // CMP_STRIDE
    l = 0
    perm = _pair_perm()

    o_q = 3 * D_CONV
    o_kv = o_q + D_ATTN
    o_gl = o_kv + 6 * KV_DIM
    w_in_l = w_in[l]
    w_conv = w_in_l[:, :o_q].astype(BF16)
    w_q = w_in_l[:, o_q:o_kv][:, perm].astype(BF16)
    w_kv = w_in_l[:, o_kv:o_gl].astype(BF16)
    w_gl = jnp.pad(w_in_l[:, o_gl:][:, _gate_perm()], ((0, 0), (0, LANES - 3 * N_HEADS))).astype(BF16)

    mod3 = _modulation(c, w_mod[l], b_mod[l]).reshape(bsz, 1, 6 * D_MODEL)
    yconv, q, cmp_kv, ksel, vsel, kwin, vwin, gates = _in_projection(
        x, mod3, g_pre_mix[l][None], w_conv, w_q, w_kv, w_gl, conv_w[l], g_conv_out[l][None])

    kvc = _compress(
        cmp_kv.reshape(bsz, nch, CMP_STRIDE * 2 * KV_DIM),
        _expand_w1(w_ck1[l][:CMP_STRIDE], w_cv1[l][:CMP_STRIDE]).astype(BF16),
        _expand_w1(w_ck1[l][CMP_STRIDE:], w_cv1[l][CMP_STRIDE:]).astype(BF16),
        _expand_pos(pos_cmp[l][:CMP_STRIDE]), _expand_pos(pos_cmp[l][CMP_STRIDE:]),
        _expand_w2(w_ck2[l], w_cv2[l]).astype(BF16))

    ck = jnp.arange(nch)[None, :] * CMP_STRIDE
    sj = jnp.arange(nsb)[:, None] * SEL_LEN
    ovt = ((ck < sj + SEL_LEN) & (ck + CMP_LEN > sj) & (ck < (nch - 1) * CMP_STRIDE)).astype(BF16)
    grp = (jnp.arange(nsb)[:, None] // (KV_TILE // SEL_LEN) == jnp.arange(LANES)[None, :]).astype(BF16)
    ocmp, selb, flag_rows = _cmp_select(q, kvc, gates, ovt, grp)
    n_words = pl.cdiv(seq // KV_TILE, FLAG_BITS)
    flags = flag_rows[:, :, :N_KV, :n_words].reshape(-1)

    et = (jnp.arange(seq)[:, None] // SEL_LEN == jnp.arange(nsb)[None, :]).astype(BF16)
    yattn = _sel_win(flags, q, ksel, vsel, kwin, vwin, selb, et, gates, ocmp, g_attn_out[l][perm][None])

    w_out_l = w_out[l]
    return _out_ffn(x, yconv, yattn, mod3, w_out_l[:D_CONV].astype(BF16), w_out_l[D_CONV:][perm].astype(BF16),
                    g_post_mix[l][None], g_pre_ffn[l][None], w_gate[l].astype(BF16), w_up[l].astype(BF16),
                    w_down[l].astype(BF16), g_post_ffn[l][None])
```

```python
import functools
import math

import jax
import jax.numpy as jnp
from jax import lax
from jax.experimental import pallas as pl
from jax.experimental.pallas import tpu as pltpu

F32 = jnp.float32
BF16 = jnp.bfloat16

D_MODEL = 1024
D_CONV = 512
CONV_WIDTH = 3
N_HEADS = 8
HEAD_DIM = 64
N_KV = 2
HPG = N_HEADS // N_KV
D_ATTN = N_HEADS * HEAD_DIM
KV_DIM = N_KV * HEAD_DIM
CMP_LEN = 32
CMP_STRIDE = 16
CMP_HIDDEN = 2 * HEAD_DIM
SEL_LEN = 64
TOP_N = 16
WINDOW = 512
D_FF = 2816
NEG = -1e30
FORCE = 1e9
EPS = 1e-6

LANES = 128
Q_TILE = 128
KV_TILE = 512
FLAG_BITS = 16
WIN_SPAN = WINDOW + Q_TILE
ROW_TILE = 512
FF_CHUNK = 256
VMEM_LIMIT = 56 * 1024 * 1024

SLOPES = [[2.0 ** (-(g * HPG + n + 1)) for n in range(HPG)] for g in range(N_KV)]


def _const_spec(shape):
    nd = len(shape)
    return pl.BlockSpec(shape, lambda *_: (0,) * nd, pipeline_mode=pl.Buffered(1))


def _rms(v):
    return v * lax.rsqrt(jnp.mean(v * v, axis=-1, keepdims=True) + EPS)


def _dot_nt(a, b):
    return lax.dot_general(a, b, (((1,), (1,)), ((), ())), preferred_element_type=F32)


def _mod_kernel(c_ref, w_ref, b_ref, o_ref):
    c = c_ref[...]
    a = c * jax.nn.sigmoid(c)
    o_ref[...] = jnp.dot(a, w_ref[...], preferred_element_type=F32,
                         precision=lax.Precision.HIGHEST) + b_ref[...]


def _modulation(c, w_mod, b_mod):
    bsz = c.shape[0]
    n_out = w_mod.shape[1]
    blk = D_MODEL
    return pl.pallas_call(
        _mod_kernel,
        grid=(n_out // blk,),
        in_specs=[pl.BlockSpec((bsz, D_MODEL), lambda j: (0, 0)),
                  pl.BlockSpec((D_MODEL, blk), lambda j: (0, j)),
                  pl.BlockSpec((1, blk), lambda j: (0, j))],
        out_specs=pl.BlockSpec((bsz, blk), lambda j: (0, j)),
        out_shape=jax.ShapeDtypeStruct((bsz, n_out), F32),
        compiler_params=pltpu.CompilerParams(dimension_semantics=("arbitrary",),
                                             vmem_limit_bytes=VMEM_LIMIT),
        name="adaln_mod",
    )(c, w_mod, b_mod.reshape(1, n_out))


def _aug_keys(k, feat):
    low = lax.broadcasted_iota(jnp.int32, (1, LANES), 1) < HEAD_DIM
    return jnp.concatenate([jnp.where(low, k, feat), jnp.where(low, feat, k)], axis=1).astype(BF16)


def _inproj_kernel(x_ref, mod_ref, gpre_ref, wconv_ref, wq_ref, wkv_ref, wgl_ref, convw_ref, gconv_ref,
                   yconv_ref, q_ref, cmp_ref, ksel_ref, vsel_ref, kwin_ref, vwin_ref, gates_ref,
                   zbuf_ref):
    tm = x_ref.shape[1]
    x = x_ref[0]
    sh = mod_ref[0, :, 0:D_MODEL]
    sc = mod_ref[0, :, D_MODEL:2 * D_MODEL]
    h = (_rms(x) * gpre_ref[...] * (1.0 + sc) + sh).astype(BF16)

    u = jnp.dot(h, wconv_ref[...], preferred_element_type=F32)
    z = u[:, D_CONV:2 * D_CONV] * u[:, 2 * D_CONV:]

    @pl.when(pl.program_id(1) == 0)
    def _():
        zbuf_ref[0:8, :] = jnp.zeros((8, D_CONV), F32)

    zbuf_ref[8:8 + tm, :] = z
    zc = (convw_ref[0:1, :] * zbuf_ref[6:6 + tm, :] + convw_ref[1:2, :] * zbuf_ref[7:7 + tm, :]
          + convw_ref[2:3, :] * z)
    zbuf_ref[0:8, :] = zbuf_ref[tm:tm + 8, :]
    yconv_ref[0] = (_rms(u[:, 0:D_CONV] * zc) * gconv_ref[...]).astype(BF16)

    q_ref[0] = (jnp.dot(h, wq_ref[...], preferred_element_type=F32) * (HEAD_DIM ** -0.5)).astype(BF16)
    kv = jnp.dot(h, wkv_ref[...], preferred_element_type=F32)
    cmp_ref[0] = kv[:, 0:2 * KV_DIM].astype(BF16)
    vsel_ref[0] = kv[:, 3 * KV_DIM:4 * KV_DIM].T.astype(BF16)
    vwin_ref[0] = kv[:, 5 * KV_DIM:6 * KV_DIM].T.astype(BF16)
    kpos = pl.program_id(1) * tm + lax.broadcasted_iota(jnp.int32, (tm, LANES), 0)
    fl = lax.broadcasted_iota(jnp.int32, (tm, LANES), 1) % HEAD_DIM
    feat = jnp.where(fl == 0, kpos // SEL_LEN,
                     jnp.where(fl == 1, kpos % SEL_LEN, jnp.where(fl < 4, 1, 0))).astype(F32)
    ksel_ref[0] = _aug_keys(kv[:, 2 * KV_DIM:3 * KV_DIM], feat)
    kwin_ref[0] = _aug_keys(kv[:, 4 * KV_DIM:5 * KV_DIM], feat)
    gates_ref[0] = jax.nn.sigmoid(jnp.dot(h, wgl_ref[...], preferred_element_type=F32))


def _in_projection(x, mod3, g_pre, w_conv, w_q, w_kv, w_gl, conv_w, g_conv):
    bsz, seq, _ = x.shape
    tm = min(ROW_TILE, seq)
    row = lambda width: pl.BlockSpec((1, tm, width), lambda b, j: (b, j, 0))
    col = pl.BlockSpec((1, KV_DIM, tm), lambda b, j: (b, 0, j))
    widths = [(D_CONV, BF16), (D_ATTN, BF16), (2 * KV_DIM, BF16), (2 * LANES, BF16), None,
              (2 * LANES, BF16), None, (LANES, F32)]
    out_shapes = [jax.ShapeDtypeStruct((bsz, KV_DIM, seq), BF16) if w is None else
                  jax.ShapeDtypeStruct((bsz, seq, w[0]), w[1]) for w in widths]
    out_specs = [col if w is None else row(w[0]) for w in widths]
    return pl.pallas_call(
        _inproj_kernel,
        grid=(bsz, seq // tm),
        in_specs=[row(D_MODEL),
                  pl.BlockSpec((1, 1, mod3.shape[2]), lambda b, j: (b, 0, 0)),
                  _const_spec(g_pre.shape), _const_spec(w_conv.shape), _const_spec(w_q.shape),
                  _const_spec(w_kv.shape), _const_spec(w_gl.shape), _const_spec(conv_w.shape),
                  _const_spec(g_conv.shape)],
        out_specs=out_specs,
        out_shape=out_shapes,
        scratch_shapes=[pltpu.VMEM((tm + 8, D_CONV), F32)],
        compiler_params=pltpu.CompilerParams(dimension_semantics=("arbitrary", "arbitrary"),
                                             vmem_limit_bytes=VMEM_LIMIT),
        name="in_projection",
    )(x, mod3, g_pre, w_conv, w_q, w_kv, w_gl, conv_w, g_conv)


def _gelu_tanh(v):
    return v * (0.5 * (1.0 + jnp.tanh(math.sqrt(2.0 / math.pi) * (v + 0.044715 * (v * v * v)))))


def _compress_kernel(x_ref, wa_ref, wb_ref, pa_ref, pb_ref, w2_ref, o_ref):
    x = x_ref[0]
    nch = x.shape[0]
    wa = wa_ref[...]
    wb = wb_ref[...]
    def pos_bias(p_ref, w):
        hi = p_ref[...].astype(BF16)
        lo = (p_ref[...] - hi.astype(F32)).astype(BF16)
        return jnp.dot(hi, w, preferred_element_type=F32) + jnp.dot(lo, w, preferred_element_type=F32)

    bias = pos_bias(pa_ref, wa) + pos_bias(pb_ref, wb)
    first = jnp.dot(x, wa, preferred_element_type=F32)
    second = jnp.dot(x, wb, preferred_element_type=F32)
    hid = _gelu_tanh(first + pltpu.roll(second, nch - 1, 0) + bias[0:1, :])
    o_ref[0] = jnp.dot(hid.astype(BF16), w2_ref[...], preferred_element_type=F32).astype(BF16)


def _compress(cmp_chunks, wa, wb, pa, pb, w2):
    bsz, nch, width = cmp_chunks.shape
    return pl.pallas_call(
        _compress_kernel,
        grid=(bsz,),
        in_specs=[pl.BlockSpec((1, nch, width), lambda b: (b, 0, 0)),
                  _const_spec(wa.shape), _const_spec(wb.shape), _const_spec(pa.shape),
                  _const_spec(pb.shape), _const_spec(w2.shape)],
        out_specs=pl.BlockSpec((1, nch, 2 * KV_DIM), lambda b: (b, 0, 0)),
        out_shape=jax.ShapeDtypeStruct((bsz, nch, 2 * KV_DIM), BF16),
        compiler_params=pltpu.CompilerParams(dimension_semantics=("arbitrary",),
                                             vmem_limit_bytes=VMEM_LIMIT),
        name="compress_kv",
    )(cmp_chunks, wa, wb, pa, pb, w2)


def _group_lhs(qp, g):
    lane = lax.broadcasted_iota(jnp.int32, (1, LANES), 1)
    keep = (lane >= g * HEAD_DIM) & (lane < (g + 1) * HEAD_DIM)
    zero = jnp.zeros((), qp.dtype)
    return jnp.concatenate(
        [jnp.where(keep, qp[:, n * LANES:(n + 1) * LANES], zero) for n in range(HPG)], axis=0)


def _pair_merge(rows_g0, rows_g1, n):
    lane = lax.broadcasted_iota(jnp.int32, (1, LANES), 1)
    return jnp.where(lane < HEAD_DIM, rows_g0[n * Q_TILE:(n + 1) * Q_TILE],
                     rows_g1[n * Q_TILE:(n + 1) * Q_TILE])


def _pair_gate(gates, branch, n):
    lane = lax.broadcasted_iota(jnp.int32, (1, LANES), 1)
    c0 = branch * N_HEADS + n
    c1 = branch * N_HEADS + HPG + n
    g0 = jnp.broadcast_to(gates[:, c0:c0 + 1], (Q_TILE, LANES))
    g1 = jnp.broadcast_to(gates[:, c1:c1 + 1], (Q_TILE, LANES))
    return jnp.where(lane < HEAD_DIM, g0, g1)


def _cmp_select_kernel(q_ref, kvc_ref, gates_ref, ovt_ref, grp_ref, ocmp_ref, selb_ref, flags_ref):
    q0 = pl.program_id(1) * Q_TILE
    flags_ref[0, 0] = jnp.zeros((8, LANES), jnp.int32)
    qp = q_ref[0]
    kc = kvc_ref[0, :, 0:KV_DIM]
    vc = kvc_ref[0, :, KV_DIM:2 * KV_DIM]
    ncp = kc.shape[0]
    nsb = ovt_ref.shape[0]

    qi = lax.broadcasted_iota(jnp.int32, (Q_TILE, ncp), 0)
    ki = lax.broadcasted_iota(jnp.int32, (Q_TILE, ncp), 1)
    dist_i = (q0 + qi) - (ki * CMP_STRIDE + (CMP_LEN - 1))
    valid = dist_i >= 0
    dist = dist_i.astype(F32)
    t_col = q0 + lax.broadcasted_iota(jnp.int32, (Q_TILE, 1), 0)
    any_valid = (t_col >= CMP_LEN - 1).astype(F32)

    blk = lax.broadcasted_iota(jnp.int32, (nsb, Q_TILE), 0)
    cur = (q0 + lax.broadcasted_iota(jnp.int32, (nsb, Q_TILE), 1)) // SEL_LEN
    forced = (blk == 0) | (blk == cur) | (blk == cur - 1)
    causal = blk <= cur
    blk_f = blk.astype(F32)

    o_rows = []
    for g in range(N_KV):
        s = _dot_nt(_group_lhs(qp, g), kc)
        p_heads = []
        for n in range(HPG):
            sn = jnp.where(valid, s[n * Q_TILE:(n + 1) * Q_TILE] - SLOPES[g][n] * dist, NEG)
            e = jnp.exp(sn - jnp.max(sn, axis=-1, keepdims=True))
            p_heads.append(e * (any_valid / jnp.sum(e, axis=-1, keepdims=True)))
        o_rows.append(jnp.dot(jnp.concatenate(p_heads, axis=0).astype(BF16), vc,
                              preferred_element_type=F32))

        p_sum = p_heads[0] + p_heads[1] + p_heads[2] + p_heads[3]
        p_hi = p_sum.astype(BF16)
        p_lo = (p_sum - p_hi.astype(F32)).astype(BF16)
        imp_t = _dot_nt(ovt_ref[...], p_hi) + _dot_nt(ovt_ref[...], p_lo)
        score = jnp.where(causal, jnp.where(forced, FORCE, imp_t), NEG)

        def pick(_, carry):
            sc, sel = carry
            top = jnp.max(sc, axis=0, keepdims=True)
            first = jnp.min(jnp.where(sc == top, blk_f, float(nsb)), axis=0, keepdims=True)
            hit = blk_f == first
            return jnp.where(hit, -jnp.inf, sc), jnp.where(hit, 1.0, sel)

        _, sel_t = lax.fori_loop(0, min(TOP_N, nsb), pick, (score, jnp.zeros((nsb, Q_TILE), F32)))
        sel = jnp.where(blk < q0 // SEL_LEN, sel_t, 0.0).T
        selb_ref[0, :, g * nsb:(g + 1) * nsb] = jnp.where(sel > 0.5, 0.0, NEG).astype(BF16)

        col_any = jnp.broadcast_to(jnp.max(sel, axis=0, keepdims=True), (8, nsb)).astype(BF16)
        hit = jnp.dot(col_any, grp_ref[...], preferred_element_type=F32) > 0.0
        lane8 = lax.broadcasted_iota(jnp.int32, (8, LANES), 1)
        weighted = jnp.where(hit, jnp.left_shift(1, lane8 % FLAG_BITS).astype(F32), 0.0)
        words = jnp.zeros((8, LANES), F32)
        for w in range(LANES // FLAG_BITS):
            word = jnp.sum(jnp.where(lane8 // FLAG_BITS == w, weighted, 0.0), axis=1, keepdims=True)
            words = jnp.where(lane8 == w, word, words)
        flags_ref[0, 0, g:g + 1, :] = words[0:1].astype(jnp.int32)

    gates = gates_ref[0]
    for n in range(HPG):
        ocmp_ref[0, :, n * LANES:(n + 1) * LANES] = (
            _pair_gate(gates, 0, n) * _pair_merge(o_rows[0], o_rows[1], n))


def _cmp_select(q, kvc, gates, ovt, grp):
    bsz, seq, _ = q.shape
    ncp = kvc.shape[1]
    nsb = ovt.shape[0]
    nq = seq // Q_TILE
    return pl.pallas_call(
        _cmp_select_kernel,
        grid=(bsz, nq),
        in_specs=[pl.BlockSpec((1, Q_TILE, D_ATTN), lambda b, i: (b, i, 0)),
                  pl.BlockSpec((1, ncp, 2 * KV_DIM), lambda b, i: (b, 0, 0)),
                  pl.BlockSpec((1, Q_TILE, LANES), lambda b, i: (b, i, 0)),
                  _const_spec(ovt.shape), _const_spec(grp.shape)],
        out_specs=[pl.BlockSpec((1, Q_TILE, D_ATTN), lambda b, i: (b, i, 0)),
                   pl.BlockSpec((1, Q_TILE, N_KV * nsb), lambda b, i: (b, i, 0)),
                   pl.BlockSpec((1, 1, 8, LANES), lambda b, i: (b, i, 0, 0))],
        out_shape=[jax.ShapeDtypeStruct((bsz, seq, D_ATTN), F32),
                   jax.ShapeDtypeStruct((bsz, seq, N_KV * nsb), BF16),
                   jax.ShapeDtypeStruct((bsz, nq, 8, LANES), jnp.int32)],
        compiler_params=pltpu.CompilerParams(dimension_semantics=("arbitrary", "arbitrary"),
                                             vmem_limit_bytes=VMEM_LIMIT),
        name="cmp_select",
    )(q, kvc, gates, ovt, grp)


def _softmax_pv_t(s_t, v_ext_t):
    p_t = jnp.exp(s_t - jnp.max(s_t, axis=0, keepdims=True)).astype(BF16)
    return jnp.dot(v_ext_t, p_t, preferred_element_type=F32)


def _sel_win_kernel(flags_ref, q_ref, ksel_ref, vsel_ref, kwin_ref, vwin_ref, selb_ref, et_ref, gates_ref,
                    ocmp_ref, gattn_ref, y_ref, m_ref, acc_ref, s0_ref, s1_ref, idx_ref):
    tile_id = pl.program_id(0) * pl.num_programs(1) + pl.program_id(1)
    q0 = pl.program_id(1) * Q_TILE
    qp = q_ref[0]
    nsb = et_ref.shape[1]
    n_words = pl.cdiv(ksel_ref.shape[1] // KV_TILE, FLAG_BITS)

    lane = lax.broadcasted_iota(jnp.int32, (1, LANES), 1)
    ki = lax.broadcasted_iota(jnp.int32, (Q_TILE, Q_TILE), 0)
    qi = lax.broadcasted_iota(jnp.int32, (Q_TILE, Q_TILE), 1)
    tri4 = jnp.concatenate([jnp.where(ki <= qi, 0.0, NEG)] * HPG, axis=1)
    win_start = pl.multiple_of(jnp.maximum(q0 - WINDOW, 0), Q_TILE)
    dw = ((q0 - win_start) + lax.broadcasted_iota(jnp.int32, (WIN_SPAN, Q_TILE), 1)
          - lax.broadcasted_iota(jnp.int32, (WIN_SPAN, Q_TILE), 0))
    wbias4 = jnp.concatenate([jnp.where((dw >= 0) & (dw < WINDOW), 0.0, NEG)] * HPG, axis=1)
    ones_q = jnp.ones((LANES, Q_TILE), BF16)
    ones_kv = jnp.ones((LANES, KV_TILE), BF16)
    ones_w = jnp.ones((LANES, WIN_SPAN), BF16)

    t = q0 + lax.broadcasted_iota(jnp.int32, (Q_TILE, LANES), 0)
    t_hi = (t // SEL_LEN).astype(F32)
    t_lo = (t % SEL_LEN).astype(F32)
    fl = lax.broadcasted_iota(jnp.int32, (Q_TILE, LANES), 1) % HEAD_DIM
    n_past = (q0 + KV_TILE - 1) // KV_TILE

    o_sel = []
    o_win = []
    for g in range(N_KV):
        keep = (lane >= g * HEAD_DIM) & (lane < (g + 1) * HEAD_DIM)
        heads = []
        for n in range(HPG):
            sl = SLOPES[g][n]
            feat = jnp.where(fl == 0, sl * SEL_LEN,
                             jnp.where(fl == 1, sl,
                                       jnp.where(fl == 2, -(sl * SEL_LEN) * t_hi,
                                                 jnp.where(fl == 3, -sl * t_lo, 0.0))))
            heads.append(jnp.where(keep, qp[:, n * LANES:(n + 1) * LANES], feat.astype(BF16)))
        lhs = jnp.concatenate(heads, axis=0)
        ks = slice(g * LANES, (g + 1) * LANES)

        s_t = _dot_nt(ksel_ref[0, pl.ds(q0, Q_TILE), ks], lhs) + tri4
        m0 = jnp.max(s_t, axis=0, keepdims=True)
        p_t = jnp.exp(s_t - m0).astype(BF16)
        acc_ref[...] = jnp.dot(jnp.concatenate([vsel_ref[0, :, pl.ds(q0, Q_TILE)], ones_q], axis=0), p_t,
                               preferred_element_type=F32)
        m_ref[...] = m0

        sel_bias = selb_ref[0, :, g * nsb:(g + 1) * nsb]
        lhs_aug = jnp.concatenate([lhs, jnp.concatenate([sel_bias] * HPG, axis=0)], axis=1)

        def note_tile(kt, cnt):
            word = flags_ref[(tile_id * N_KV + g) * n_words + kt // FLAG_BITS]
            idx_ref[cnt] = kt
            return cnt + ((word >> (kt % FLAG_BITS)) & 1)

        count = lax.fori_loop(0, n_past, note_tile, 0)

        def scores(slot, s_ref):
            k0 = pl.multiple_of(idx_ref[slot] * KV_TILE, KV_TILE)
            keys = jnp.concatenate([ksel_ref[0, pl.ds(k0, KV_TILE), ks], et_ref[pl.ds(k0, KV_TILE), :]], axis=1)
            s_ref[...] = _dot_nt(keys, lhs_aug)

        def absorb(slot, s_ref):
            k0 = pl.multiple_of(idx_ref[slot] * KV_TILE, KV_TILE)
            s_t = s_ref[...]
            m_old = m_ref[...]
            m_new = jnp.maximum(m_old, jnp.max(s_t, axis=0, keepdims=True))
            p_t = jnp.exp(s_t - m_new).astype(BF16)
            pv = jnp.dot(jnp.concatenate([vsel_ref[0, :, pl.ds(k0, KV_TILE)], ones_kv], axis=0), p_t,
                         preferred_element_type=F32)
            acc_ref[...] = jnp.exp(m_old - m_new) * acc_ref[...] + pv
            m_ref[...] = m_new

        @pl.when(count > 0)
        def _():
            scores(0, s0_ref)

        def pair_step(jj, carry):
            scores(2 * jj + 1, s1_ref)
            absorb(2 * jj, s0_ref)
            scores(jnp.minimum(2 * jj + 2, count - 1), s0_ref)
            absorb(2 * jj + 1, s1_ref)
            return carry

        lax.fori_loop(0, count // 2, pair_step, 0)

        @pl.when(count % 2 == 1)
        def _():
            absorb(count - 1, s0_ref)

        acc = acc_ref[...]
        o_sel.append(acc[g * HEAD_DIM:(g + 1) * HEAD_DIM] / acc[LANES:LANES + HEAD_DIM])

        s_t = _dot_nt(kwin_ref[0, pl.ds(win_start, WIN_SPAN), ks], lhs) + wbias4
        ow = _softmax_pv_t(s_t, jnp.concatenate([vwin_ref[0, :, pl.ds(win_start, WIN_SPAN)], ones_w], axis=0))
        o_win.append(ow[g * HEAD_DIM:(g + 1) * HEAD_DIM] / ow[LANES:LANES + HEAD_DIM])

    gates_t = gates_ref[0].T
    low = lane < HEAD_DIM
    for n in range(HPG):
        qs = slice(n * Q_TILE, (n + 1) * Q_TILE)
        halves = []
        for g in range(N_KV):
            h = g * HPG + n
            halves.append(gates_t[N_HEADS + h:N_HEADS + h + 1] * o_sel[g][:, qs]
                          + gates_t[2 * N_HEADS + h:2 * N_HEADS + h + 1] * o_win[g][:, qs])
        y = ocmp_ref[0, :, n * LANES:(n + 1) * LANES] + jnp.concatenate(halves, axis=0).T
        y2 = y * y
        ms_lo = jnp.sum(jnp.where(low, y2, 0.0), axis=-1, keepdims=True) * (1.0 / HEAD_DIM)
        ms_hi = jnp.sum(jnp.where(low, 0.0, y2), axis=-1, keepdims=True) * (1.0 / HEAD_DIM)
        inv = jnp.where(low, lax.rsqrt(ms_lo + EPS), lax.rsqrt(ms_hi + EPS))
        y_ref[0, :, n * LANES:(n + 1) * LANES] = (
            y * inv * gattn_ref[:, n * LANES:(n + 1) * LANES]).astype(BF16)


def _sel_win(flags, q, ksel, vsel, kwin, vwin, selb, et, gates, ocmp, g_attn):
    bsz, seq, _ = q.shape
    nsb = et.shape[1]
    tile = lambda width: pl.BlockSpec((1, Q_TILE, width), lambda b, i, *_: (b, i, 0))
    whole = lambda width: pl.BlockSpec((1, seq, width), lambda b, i, *_: (b, 0, 0))
    whole_t = pl.BlockSpec((1, KV_DIM, seq), lambda b, i, *_: (b, 0, 0))
    cols = HPG * Q_TILE
    return pl.pallas_call(
        _sel_win_kernel,
        grid_spec=pltpu.PrefetchScalarGridSpec(
            num_scalar_prefetch=1,
            grid=(bsz, seq // Q_TILE),
            in_specs=[tile(D_ATTN), whole(2 * LANES), whole_t, whole(2 * LANES), whole_t,
                      tile(N_KV * nsb), _const_spec(et.shape), tile(LANES), tile(D_ATTN),
                      _const_spec(g_attn.shape)],
            out_specs=tile(D_ATTN),
            scratch_shapes=[pltpu.VMEM((1, cols), F32), pltpu.VMEM((2 * LANES, cols), F32),
                            pltpu.VMEM((KV_TILE, cols), F32), pltpu.VMEM((KV_TILE, cols), F32),
                            pltpu.SMEM((seq // KV_TILE,), jnp.int32)]),
        out_shape=jax.ShapeDtypeStruct((bsz, seq, D_ATTN), BF16),
        compiler_params=pltpu.CompilerParams(dimension_semantics=("arbitrary", "arbitrary"),
                                             vmem_limit_bytes=VMEM_LIMIT),
        name="sel_win_attn",
    )(flags, q, ksel, vsel, kwin, vwin, selb, et, gates, ocmp, g_attn)


def _out_ffn_kernel(x_ref, yconv_ref, yattn_ref, mod_ref, woc_ref, woa_ref, gpm_ref, gpf_ref,
                    wg_ref, wu_ref, wd_ref, gpo_ref, o_ref, f_ref):
    D = D_MODEL
    ga_m = mod_ref[0, :, 2 * D:3 * D]
    sh_f = mod_ref[0, :, 3 * D:4 * D]
    sc_f = mod_ref[0, :, 4 * D:5 * D]
    ga_f = mod_ref[0, :, 5 * D:6 * D]
    y = (jnp.dot(yconv_ref[0], woc_ref[...], preferred_element_type=F32)
         + jnp.dot(yattn_ref[0], woa_ref[...], preferred_element_type=F32))
    x1 = x_ref[0] + ga_m * (_rms(y) * gpm_ref[...])
    h = (_rms(x1) * gpf_ref[...] * (1.0 + sc_f) + sh_f).astype(BF16)
    for c in range(D_FF // FF_CHUNK):
        cols = slice(c * FF_CHUNK, (c + 1) * FF_CHUNK)
        gate = jnp.dot(h, wg_ref[:, cols], preferred_element_type=F32)
        up = jnp.dot(h, wu_ref[:, cols], preferred_element_type=F32)
        act = (gate * jax.nn.sigmoid(gate) * up).astype(BF16)
        part = jnp.dot(act, wd_ref[cols, :], preferred_element_type=F32)
        if c == 0:
            f_ref[...] = part
        else:
            f_ref[...] += part
    o_ref[0] = x1 + ga_f * (_rms(f_ref[...]) * gpo_ref[...])


def _out_ffn(x, yconv, yattn, mod3, wo_c, wo_a, g_pm, g_pf, wg, wu, wd, g_po):
    bsz, seq, _ = x.shape
    tm = min(ROW_TILE, seq)
    row = lambda width: pl.BlockSpec((1, tm, width), lambda b, j: (b, j, 0))
    return pl.pallas_call(
        _out_ffn_kernel,
        grid=(bsz, seq // tm),
        in_specs=[row(D_MODEL), row(D_CONV), row(D_ATTN),
                  pl.BlockSpec((1, 1, mod3.shape[2]), lambda b, j: (b, 0, 0)),
                  _const_spec(wo_c.shape), _const_spec(wo_a.shape), _const_spec(g_pm.shape),
                  _const_spec(g_pf.shape), _const_spec(wg.shape), _const_spec(wu.shape),
                  _const_spec(wd.shape), _const_spec(g_po.shape)],
        out_specs=row(D_MODEL),
        out_shape=jax.ShapeDtypeStruct((bsz, seq, D_MODEL), F32),
        scratch_shapes=[pltpu.VMEM((tm, D_MODEL), F32)],
        compiler_params=pltpu.CompilerParams(dimension_semantics=("arbitrary", "arbitrary"),
                                             vmem_limit_bytes=VMEM_LIMIT),
        name="out_ffn",
    )(x, yconv, yattn, mod3, wo_c, wo_a, g_pm, g_pf, wg, wu, wd, g_po)


def _pair_perm():
    idx = []
    for n in range(HPG):
        for g in range(N_KV):
            base = g * HPG * HEAD_DIM + n * HEAD_DIM
            idx.extend(range(base, base + HEAD_DIM))
    return jnp.array(idx, jnp.int32)


def _gate_perm():
    return jnp.array([h * 3 + j for j in range(3) for h in range(N_HEADS)], jnp.int32)


def _expand_w1(w1_half_k, w1_half_v):
    eye_g = jnp.eye(N_KV, dtype=F32)
    blocks = []
    for kv, w in enumerate((w1_half_k, w1_half_v)):
        e = jnp.einsum("ldh,gk->lgdkh", w, eye_g).reshape(CMP_STRIDE, KV_DIM, N_KV * CMP_HIDDEN)
        pad = jnp.zeros_like(e)
        blocks.append(jnp.concatenate([e, pad] if kv == 0 else [pad, e], axis=2))
    return jnp.concatenate(blocks, axis=1).reshape(CMP_STRIDE * 2 * KV_DIM, 2 * N_KV * CMP_HIDDEN)


def _expand_pos(pos_half):
    rowv = jnp.tile(pos_half[:, None, None, :], (1, 2, N_KV, 1)).reshape(1, -1)
    return jnp.tile(rowv, (8, 1))


def _expand_w2(w2_k, w2_v):
    eye_g = jnp.eye(N_KV, dtype=F32)
    out = []
    for kv, w in enumerate((w2_k, w2_v)):
        e = jnp.einsum("hd,gk->ghkd", w, eye_g).reshape(N_KV * CMP_HIDDEN, KV_DIM)
        pad = jnp.zeros_like(e)
        out.append(jnp.concatenate([e, pad] if kv == 0 else [pad, e], axis=1))
    return jnp.concatenate(out, axis=0)


def kernel(x, c, w_mod, b_mod, g_pre_mix, w_in, conv_w, pos_cmp, w_ck1, w_ck2, w_cv1, w_cv2, g_conv_out,
           g_attn_out, w_out, g_post_mix, g_pre_ffn, w_gate, w_up, w_down, g_post_ffn):
    bsz, seq, _ = x.shape
    assert seq % KV_TILE == 0 and seq >= WIN_SPAN and w_mod.shape[0] == 1
    nsb = seq // SEL_LEN
    nch = seq // CMP_STRIDE
    l = 0
    perm = _pair_perm()

    o_q = 3 * D_CONV
    o_kv = o_q + D_ATTN
    o_gl = o_kv + 6 * KV_DIM
    w_in_l = w_in[l]
    w_conv = w_in_l[:, :o_q].astype(BF16)
    w_q = w_in_l[:, o_q:o_kv][:, perm].astype(BF16)
    w_kv = w_in_l[:, o_kv:o_gl].astype(BF16)
    w_gl = jnp.pad(w_in_l[:, o_gl:][:, _gate_perm()], ((0, 0), (0, LANES - 3 * N_HEADS))).astype(BF16)

    mod3 = _modulation(c, w_mod[l], b_mod[l]).reshape(bsz, 1, 6 * D_MODEL)
    yconv, q, cmp_kv, ksel, vsel, kwin, vwin, gates = _in_projection(
        x, mod3, g_pre_mix[l][None], w_conv, w_q, w_kv, w_gl, conv_w[l], g_conv_out[l][None])

    kvc = _compress(
        cmp_kv.reshape(bsz, nch, CMP_STRIDE * 2 * KV_DIM),
        _expand_w1(w_ck1[l][:CMP_STRIDE], w_cv1[l][:CMP_STRIDE]).astype(BF16),
        _expand_w1(w_ck1[l][CMP_STRIDE:], w_cv1[l][CMP_STRIDE:]).astype(BF16),
        _expand_pos(pos_cmp[l][:CMP_STRIDE]), _expand_pos(pos_cmp[l][CMP_STRIDE:]),
        _expand_w2(w_ck2[l], w_cv2[l]).astype(BF16))

    ck = jnp.arange(nch)[None, :] * CMP_STRIDE
    sj = jnp.arange(nsb)[:, None] * SEL_LEN
    ovt = ((ck < sj + SEL_LEN) & (ck + CMP_LEN > sj) & (ck < (nch - 1) * CMP_STRIDE)).astype(BF16)
    grp = (jnp.arange(nsb)[:, None] // (KV_TILE // SEL_LEN) == jnp.arange(LANES)[None, :]).astype(BF16)
    ocmp, selb, flag_rows = _cmp_select(q, kvc, gates, ovt, grp)
    n_words = pl.cdiv(seq // KV_TILE, FLAG_BITS)
    flags = flag_rows[:, :, :N_KV, :n_words].reshape(-1)

    et = (jnp.arange(seq)[:, None] // SEL_LEN == jnp.arange(nsb)[None, :]).astype(BF16)
    yattn = _sel_win(flags, q, ksel, vsel, kwin, vwin, selb, et, gates, ocmp, g_attn_out[l][perm][None])

    w_out_l = w_out[l]
    return _out_ffn(x, yconv, yattn, mod3, w_out_l[:D_CONV].astype(BF16), w_out_l[D_CONV:][perm].astype(BF16),
                    g_post_mix[l][None], g_pre_ffn[l][None], w_gate[l].astype(BF16), w_up[l].astype(BF16),
                    w_down[l].astype(BF16), g_post_ffn[l][None])
```

```python
import functools
import math

import jax
import jax.numpy as jnp
from jax import lax
from jax.experimental import pallas as pl
from jax.experimental.pallas import tpu as pltpu

F32 = jnp.float32
BF16 = jnp.bfloat16

D_MODEL = 1024
D_CONV = 512
CONV_WIDTH = 3
N_HEADS = 8
HEAD_DIM = 64
N_KV = 2
HPG = N_HEADS // N_KV
D_ATTN = N_HEADS * HEAD_DIM
KV_DIM = N_KV * HEAD_DIM
CMP_LEN = 32
CMP_STRIDE = 16
CMP_HIDDEN = 2 * HEAD_DIM
SEL_LEN = 64
TOP_N = 16
WINDOW = 512
D_FF = 2816
NEG = -1e30
FORCE = 1e9
EPS = 1e-6

LANES = 128
Q_TILE = 128
KV_TILE = 512
FLAG_BITS = 16
CMP_CHUNK = 128
WIN_SPAN = WINDOW + Q_TILE
ROW_TILE = 512
FF_CHUNK = 256
VMEM_LIMIT = 56 * 1024 * 1024

SLOPES = [[2.0 ** (-(g * HPG + n + 1)) for n in range(HPG)] for g in range(N_KV)]


def _const_spec(shape):
    nd = len(shape)
    return pl.BlockSpec(shape, lambda *_: (0,) * nd, pipeline_mode=pl.Buffered(1))


def _rms(v):
    return v * lax.rsqrt(jnp.mean(v * v, axis=-1, keepdims=True) + EPS)


def _dot_nt(a, b):
    return lax.dot_general(a, b, (((1,), (1,)), ((), ())), preferred_element_type=F32)


def _mod_kernel(c_ref, w_ref, b_ref, o_ref):
    c = c_ref[...]
    a = c * jax.nn.sigmoid(c)
    o_ref[...] = jnp.dot(a, w_ref[...], preferred_element_type=F32,
                         precision=lax.Precision.HIGHEST) + b_ref[...]


def _modulation(c, w_mod, b_mod):
    bsz = c.shape[0]
    n_out = w_mod.shape[1]
    blk = D_MODEL
    return pl.pallas_call(
        _mod_kernel,
        grid=(n_out // blk,),
        in_specs=[pl.BlockSpec((bsz, D_MODEL), lambda j: (0, 0)),
                  pl.BlockSpec((D_MODEL, blk), lambda j: (0, j)),
                  pl.BlockSpec((1, blk), lambda j: (0, j))],
        out_specs=pl.BlockSpec((bsz, blk), lambda j: (0, j)),
        out_shape=jax.ShapeDtypeStruct((bsz, n_out), F32),
        compiler_params=pltpu.CompilerParams(dimension_semantics=("arbitrary",),
                                             vmem_limit_bytes=VMEM_LIMIT),
        name="adaln_mod",
    )(c, w_mod, b_mod.reshape(1, n_out))


def _aug_keys(k, feat):
    low = lax.broadcasted_iota(jnp.int32, (1, LANES), 1) < HEAD_DIM
    return jnp.concatenate([jnp.where(low, k, feat), jnp.where(low, feat, k)], axis=1).astype(BF16)


def _inproj_kernel(x_ref, mod_ref, gpre_ref, wconv_ref, wq_ref, wkv_ref, wgl_ref, convw_ref, gconv_ref,
                   yconv_ref, q_ref, cmp_ref, ksel_ref, vsel_ref, kwin_ref, vwin_ref, gates_ref,
                   zbuf_ref):
    tm = x_ref.shape[1]
    x = x_ref[0]
    sh = mod_ref[0, :, 0:D_MODEL]
    sc = mod_ref[0, :, D_MODEL:2 * D_MODEL]
    h = (_rms(x) * gpre_ref[...] * (1.0 + sc) + sh).astype(BF16)

    u = jnp.dot(h, wconv_ref[...], preferred_element_type=F32)
    z = u[:, D_CONV:2 * D_CONV] * u[:, 2 * D_CONV:]

    @pl.when(pl.program_id(1) == 0)
    def _():
        zbuf_ref[0:8, :] = jnp.zeros((8, D_CONV), F32)

    zbuf_ref[8:8 + tm, :] = z
    zc = (convw_ref[0:1, :] * zbuf_ref[6:6 + tm, :] + convw_ref[1:2, :] * zbuf_ref[7:7 + tm, :]
          + convw_ref[2:3, :] * z)
    zbuf_ref[0:8, :] = zbuf_ref[tm:tm + 8, :]
    yconv_ref[0] = (_rms(u[:, 0:D_CONV] * zc) * gconv_ref[...]).astype(BF16)

    q_ref[0] = (jnp.dot(h, wq_ref[...], preferred_element_type=F32) * (HEAD_DIM ** -0.5)).astype(BF16)
    kv = jnp.dot(h, wkv_ref[...], preferred_element_type=F32)
    cmp_ref[0] = kv[:, 0:2 * KV_DIM].astype(BF16)
    vsel_ref[0] = kv[:, 3 * KV_DIM:4 * KV_DIM].T.astype(BF16)
    vwin_ref[0] = kv[:, 5 * KV_DIM:6 * KV_DIM].T.astype(BF16)
    kpos = pl.program_id(1) * tm + lax.broadcasted_iota(jnp.int32, (tm, LANES), 0)
    fl = lax.broadcasted_iota(jnp.int32, (tm, LANES), 1) % HEAD_DIM
    feat = jnp.where(fl == 0, kpos // SEL_LEN,
                     jnp.where(fl == 1, kpos % SEL_LEN, jnp.where(fl < 4, 1, 0))).astype(F32)
    ksel_ref[0] = _aug_keys(kv[:, 2 * KV_DIM:3 * KV_DIM], feat)
    kwin_ref[0] = _aug_keys(kv[:, 4 * KV_DIM:5 * KV_DIM], feat)
    gates_ref[0] = jax.nn.sigmoid(jnp.dot(h, wgl_ref[...], preferred_element_type=F32))


def _in_projection(x, mod3, g_pre, w_conv, w_q, w_kv, w_gl, conv_w, g_conv):
    bsz, seq, _ = x.shape
    tm = min(ROW_TILE, seq)
    row = lambda width: pl.BlockSpec((1, tm, width), lambda b, j: (b, j, 0))
    col = pl.BlockSpec((1, KV_DIM, tm), lambda b, j: (b, 0, j))
    widths = [(D_CONV, BF16), (D_ATTN, BF16), (2 * KV_DIM, BF16), (2 * LANES, BF16), None,
              (2 * LANES, BF16), None, (LANES, F32)]
    out_shapes = [jax.ShapeDtypeStruct((bsz, KV_DIM, seq), BF16) if w is None else
                  jax.ShapeDtypeStruct((bsz, seq, w[0]), w[1]) for w in widths]
    out_specs = [col if w is None else row(w[0]) for w in widths]
    return pl.pallas_call(
        _inproj_kernel,
        grid=(bsz, seq // tm),
        in_specs=[row(D_MODEL),
                  pl.BlockSpec((1, 1, mod3.shape[2]), lambda b, j: (b, 0, 0)),
                  _const_spec(g_pre.shape), _const_spec(w_conv.shape), _const_spec(w_q.shape),
                  _const_spec(w_kv.shape), _const_spec(w_gl.shape), _const_spec(conv_w.shape),
                  _const_spec(g_conv.shape)],
        out_specs=out_specs,
        out_shape=out_shapes,
        scratch_shapes=[pltpu.VMEM((tm + 8, D_CONV), F32)],
        compiler_params=pltpu.CompilerParams(dimension_semantics=("arbitrary", "arbitrary"),
                                             vmem_limit_bytes=VMEM_LIMIT),
        name="in_projection",
    )(x, mod3, g_pre, w_conv, w_q, w_kv, w_gl, conv_w, g_conv)


def _gelu_tanh(v):
    return v * (0.5 * (1.0 + jnp.tanh(math.sqrt(2.0 / math.pi) * (v + 0.044715 * (v * v * v)))))


def _compress_kernel(x_ref, wa_ref, wb_ref, pa_ref, pb_ref, w2_ref, kc_ref, vct_ref):
    x = x_ref[0]
    nch = x.shape[0]
    wa = wa_ref[...]
    wb = wb_ref[...]
    def pos_bias(p_ref, w):
        hi = p_ref[...].astype(BF16)
        lo = (p_ref[...] - hi.astype(F32)).astype(BF16)
        return jnp.dot(hi, w, preferred_element_type=F32) + jnp.dot(lo, w, preferred_element_type=F32)

    bias = pos_bias(pa_ref, wa) + pos_bias(pb_ref, wb)
    first = jnp.dot(x, wa, preferred_element_type=F32)
    second = jnp.dot(x, wb, preferred_element_type=F32)
    hid = _gelu_tanh(first + pltpu.roll(second, nch - 1, 0) + bias[0:1, :])
    out = jnp.dot(hid.astype(BF16), w2_ref[...], preferred_element_type=F32)
    last = (lax.broadcasted_iota(jnp.int32, (nch, LANES), 0) * CMP_STRIDE + (CMP_LEN - 1))
    fl = lax.broadcasted_iota(jnp.int32, (nch, LANES), 1) % HEAD_DIM
    feat = jnp.where(fl == 0, last // SEL_LEN,
                     jnp.where(fl == 1, last % SEL_LEN, jnp.where(fl < 4, 1, 0))).astype(F32)
    kc_ref[0] = _aug_keys(out[:, 0:KV_DIM], feat)
    vct_ref[0] = out[:, KV_DIM:2 * KV_DIM].T.astype(BF16)


def _compress(cmp_chunks, wa, wb, pa, pb, w2):
    bsz, nch, width = cmp_chunks.shape
    return pl.pallas_call(
        _compress_kernel,
        grid=(bsz,),
        in_specs=[pl.BlockSpec((1, nch, width), lambda b: (b, 0, 0)),
                  _const_spec(wa.shape), _const_spec(wb.shape), _const_spec(pa.shape),
                  _const_spec(pb.shape), _const_spec(w2.shape)],
        out_specs=[pl.BlockSpec((1, nch, 2 * LANES), lambda b: (b, 0, 0)),
                   pl.BlockSpec((1, KV_DIM, nch), lambda b: (b, 0, 0))],
        out_shape=[jax.ShapeDtypeStruct((bsz, nch, 2 * LANES), BF16),
                   jax.ShapeDtypeStruct((bsz, KV_DIM, nch), BF16)],
        compiler_params=pltpu.CompilerParams(dimension_semantics=("arbitrary",),
                                             vmem_limit_bytes=VMEM_LIMIT),
        name="compress_kv",
    )(cmp_chunks, wa, wb, pa, pb, w2)


def _query_lhs(qp, q0):
    lane = lax.broadcasted_iota(jnp.int32, (1, LANES), 1)
    t = q0 + lax.broadcasted_iota(jnp.int32, (Q_TILE, LANES), 0)
    t_hi = (t // SEL_LEN).astype(F32)
    t_lo = (t % SEL_LEN).astype(F32)
    fl = lax.broadcasted_iota(jnp.int32, (Q_TILE, LANES), 1) % HEAD_DIM
    out = []
    for g in range(N_KV):
        keep = (lane >= g * HEAD_DIM) & (lane < (g + 1) * HEAD_DIM)
        heads = []
        for n in range(HPG):
            sl = SLOPES[g][n]
            feat = jnp.where(fl == 0, sl * SEL_LEN,
                             jnp.where(fl == 1, sl,
                                       jnp.where(fl == 2, -(sl * SEL_LEN) * t_hi,
                                                 jnp.where(fl == 3, -sl * t_lo, 0.0))))
            heads.append(jnp.where(keep, qp[:, n * LANES:(n + 1) * LANES], feat.astype(BF16)))
        out.append(jnp.concatenate(heads, axis=0))
    return out


def _head_gates(gates_t, branch, g):
    base = branch * N_HEADS + g * HPG
    return jnp.concatenate([gates_t[base + n:base + n + 1] for n in range(HPG)], axis=1)


def _cmp_select_kernel(q_ref, kc_ref, vct_ref, gates_ref, ovt_ref, grp_ref, ocmp_ref, selb_ref, flags_ref,
                       sel_ref):
    q0 = pl.program_id(1) * Q_TILE
    flags_ref[0, 0] = jnp.zeros((8, LANES), jnp.int32)
    ncp = kc_ref.shape[1]
    nsb = ovt_ref.shape[0]
    lhs_g = _query_lhs(q_ref[0], q0)
    t_row = q0 + lax.broadcasted_iota(jnp.int32, (1, Q_TILE), 1)
    any_valid = jnp.concatenate([(t_row >= CMP_LEN - 1).astype(F32)] * HPG, axis=1)
    gates_t = gates_ref[0].T

    chunk = min(ncp, CMP_CHUNK)
    reach = (q0 + Q_TILE - CMP_LEN) // (CMP_STRIDE * chunk)

    def body(n_chunks):
        nk = n_chunks * chunk
        n_mask = min(2, n_chunks) * chunk
        dist = ((q0 + lax.broadcasted_iota(jnp.int32, (n_mask, Q_TILE), 1))
                - ((nk - n_mask + lax.broadcasted_iota(jnp.int32, (n_mask, Q_TILE), 0)) * CMP_STRIDE
                   + (CMP_LEN - 1)))
        vbias4 = jnp.concatenate([jnp.where(dist >= 0, 0.0, NEG)] * HPG, axis=1)
        s_g = [_dot_nt(kc_ref[0, 0:nk, g * LANES:(g + 1) * LANES], lhs_g[g]) for g in range(N_KV)]
        nb = min(nsb, nk * CMP_STRIDE // SEL_LEN)
        lhs_vo = jnp.concatenate([vct_ref[0, :, 0:nk], ovt_ref[0:nb, 0:nk]], axis=0)
        imp = []
        for g in range(N_KV):
            s_t = s_g[g][nk - n_mask:nk] + vbias4
            if nk > n_mask:
                s_t = jnp.concatenate([s_g[g][0:nk - n_mask], s_t], axis=0)
            e = jnp.exp(s_t - jnp.max(s_t, axis=0, keepdims=True))
            inv = any_valid / jnp.sum(e, axis=0, keepdims=True)
            r = jnp.dot(lhs_vo, e.astype(BF16), preferred_element_type=F32)
            rows = slice(g * HEAD_DIM, (g + 1) * HEAD_DIM)
            ocmp_ref[0, 0, rows, :] = (_head_gates(gates_t, 0, g) * inv) * r[rows]
            ri = r[KV_DIM:KV_DIM + nb] * inv
            imp.append(ri[:, 0:Q_TILE] + ri[:, Q_TILE:2 * Q_TILE]
                       + ri[:, 2 * Q_TILE:3 * Q_TILE] + ri[:, 3 * Q_TILE:4 * Q_TILE])

        blk = lax.broadcasted_iota(jnp.int32, (nb, Q_TILE), 0)
        cur = (q0 + lax.broadcasted_iota(jnp.int32, (nb, Q_TILE), 1)) // SEL_LEN
        forced = (blk == 0) | (blk == cur) | (blk == cur - 1)
        free = (blk <= cur) & jnp.logical_not(forced)
        blk_f = blk.astype(F32)

        def pick(_, carry):
            out = []
            for sc in carry:
                top = jnp.max(sc, axis=0, keepdims=True)
                first = jnp.min(jnp.where(sc == top, blk_f, float(nb)), axis=0, keepdims=True)
                out.append(jnp.where(blk_f == first, -jnp.inf, sc))
            return tuple(out)

        picked = lax.fori_loop(0, min(TOP_N - 3, nb), pick,
                               tuple(jnp.where(free, imp[g], NEG) for g in range(N_KV)))
        past = blk < q0 // SEL_LEN
        for g in range(N_KV):
            sel_ref[g, 0:nb, :] = jnp.where(past & (forced | (picked[g] == -jnp.inf)), 1.0, 0.0)
            if nb < nsb:
                sel_ref[g, nb:nsb, :] = jnp.zeros((nsb - nb, Q_TILE), F32)

    for c in range(ncp // chunk):
        pl.when(reach == c)(functools.partial(body, c + 1))

    for g in range(N_KV):
        sel = sel_ref[g].T
        selb_ref[0, :, g * nsb:(g + 1) * nsb] = jnp.where(sel > 0.5, 0.0, NEG).astype(BF16)

        col_any = jnp.broadcast_to(jnp.max(sel, axis=0, keepdims=True), (8, nsb)).astype(BF16)
        hit = jnp.dot(col_any, grp_ref[...], preferred_element_type=F32) > 0.0
        lane8 = lax.broadcasted_iota(jnp.int32, (8, LANES), 1)
        weighted = jnp.where(hit, jnp.left_shift(1, lane8 % FLAG_BITS).astype(F32), 0.0)
        words = jnp.zeros((8, LANES), F32)
        for w in range(LANES // FLAG_BITS):
            word = jnp.sum(jnp.where(lane8 // FLAG_BITS == w, weighted, 0.0), axis=1, keepdims=True)
            words = jnp.where(lane8 == w, word, words)
        flags_ref[0, 0, g:g + 1, :] = words[0:1].astype(jnp.int32)


def _cmp_select(q, kc, vct, gates, ovt, grp):
    bsz, seq, _ = q.shape
    ncp = kc.shape[1]
    nsb = ovt.shape[0]
    nq = seq // Q_TILE
    return pl.pallas_call(
        _cmp_select_kernel,
        grid=(bsz, nq),
        in_specs=[pl.BlockSpec((1, Q_TILE, D_ATTN), lambda b, i: (b, i, 0)),
                  pl.BlockSpec((1, ncp, 2 * LANES), lambda b, i: (b, 0, 0)),
                  pl.BlockSpec((1, KV_DIM, ncp), lambda b, i: (b, 0, 0)),
                  pl.BlockSpec((1, Q_TILE, LANES), lambda b, i: (b, i, 0)),
                  _const_spec(ovt.shape), _const_spec(grp.shape)],
        out_specs=[pl.BlockSpec((1, 1, KV_DIM, HPG * Q_TILE), lambda b, i: (b, i, 0, 0)),
                   pl.BlockSpec((1, Q_TILE, N_KV * nsb), lambda b, i: (b, i, 0)),
                   pl.BlockSpec((1, 1, 8, LANES), lambda b, i: (b, i, 0, 0))],
        out_shape=[jax.ShapeDtypeStruct((bsz, nq, KV_DIM, HPG * Q_TILE), F32),
                   jax.ShapeDtypeStruct((bsz, seq, N_KV * nsb), BF16),
                   jax.ShapeDtypeStruct((bsz, nq, 8, LANES), jnp.int32)],
        scratch_shapes=[pltpu.VMEM((N_KV, nsb, Q_TILE), F32)],
        compiler_params=pltpu.CompilerParams(dimension_semantics=("arbitrary", "arbitrary"),
                                             vmem_limit_bytes=VMEM_LIMIT),
        name="cmp_select",
    )(q, kc, vct, gates, ovt, grp)


def _softmax_pv_t(s_t, v_ext_t):
    p_t = jnp.exp(s_t - jnp.max(s_t, axis=0, keepdims=True)).astype(BF16)
    return jnp.dot(v_ext_t, p_t, preferred_element_type=F32)


def _sel_win_kernel(flags_ref, q_ref, ksel_ref, vsel_ref, kwin_ref, vwin_ref, selb_ref, et_ref, gates_ref,
                    ocmp_ref, gattn_ref, y_ref, m_ref, acc_ref, ow_ref, s0_ref, s1_ref, idx_ref):
    tile_id = pl.program_id(0) * pl.num_programs(1) + pl.program_id(1)
    q0 = pl.program_id(1) * Q_TILE
    qp = q_ref[0]
    nsb = et_ref.shape[1]
    n_words = pl.cdiv(ksel_ref.shape[1] // KV_TILE, FLAG_BITS)

    lane = lax.broadcasted_iota(jnp.int32, (1, LANES), 1)
    ki = lax.broadcasted_iota(jnp.int32, (Q_TILE, Q_TILE), 0)
    qi = lax.broadcasted_iota(jnp.int32, (Q_TILE, Q_TILE), 1)
    tri4 = jnp.concatenate([jnp.where(ki <= qi, 0.0, NEG)] * HPG, axis=1)
    win_start = pl.multiple_of(jnp.maximum(q0 - WINDOW, 0), Q_TILE)
    dw = ((q0 - win_start) + lax.broadcasted_iota(jnp.int32, (WIN_SPAN, Q_TILE), 1)
          - lax.broadcasted_iota(jnp.int32, (WIN_SPAN, Q_TILE), 0))
    wbias4 = jnp.concatenate([jnp.where((dw >= 0) & (dw < WINDOW), 0.0, NEG)] * HPG, axis=1)
    ones_q = jnp.ones((LANES, Q_TILE), BF16)
    ones_kv = jnp.ones((LANES, KV_TILE), BF16)
    ones_w = jnp.ones((LANES, WIN_SPAN), BF16)

    n_past = (q0 + KV_TILE - 1) // KV_TILE
    lhs_g = _query_lhs(qp, q0)

    s_diag = [_dot_nt(ksel_ref[0, pl.ds(q0, Q_TILE), g * LANES:(g + 1) * LANES], lhs_g[g]) + tri4
              for g in range(N_KV)]
    s_win = [_dot_nt(kwin_ref[0, pl.ds(win_start, WIN_SPAN), g * LANES:(g + 1) * LANES], lhs_g[g]) + wbias4
             for g in range(N_KV)]
    vd_ext = jnp.concatenate([vsel_ref[0, :, pl.ds(q0, Q_TILE)], ones_q], axis=0)
    vw_ext = jnp.concatenate([vwin_ref[0, :, pl.ds(win_start, WIN_SPAN)], ones_w], axis=0)
    for g in range(N_KV):
        m0 = jnp.max(s_diag[g], axis=0, keepdims=True)
        acc_ref[g] = jnp.dot(vd_ext, jnp.exp(s_diag[g] - m0).astype(BF16), preferred_element_type=F32)
        m_ref[g] = m0
    for g in range(N_KV):
        ow = _softmax_pv_t(s_win[g], vw_ext)
        ow_ref[g] = ow[g * HEAD_DIM:(g + 1) * HEAD_DIM] / ow[LANES:LANES + HEAD_DIM]

    o_sel = []
    for g in range(N_KV):
        lhs = lhs_g[g]
        ks = slice(g * LANES, (g + 1) * LANES)
        m_g = m_ref.at[g]
        acc_g = acc_ref.at[g]

        sel_bias = selb_ref[0, :, g * nsb:(g + 1) * nsb]
        lhs_aug = jnp.concatenate([lhs, jnp.concatenate([sel_bias] * HPG, axis=0)], axis=1)

        def note_tile(kt, cnt):
            word = flags_ref[(tile_id * N_KV + g) * n_words + kt // FLAG_BITS]
            idx_ref[cnt] = kt
            return cnt + ((word >> (kt % FLAG_BITS)) & 1)

        count = lax.fori_loop(0, n_past, note_tile, 0)

        def scores(slot, s_ref):
            k0 = pl.multiple_of(idx_ref[slot] * KV_TILE, KV_TILE)
            keys = jnp.concatenate([ksel_ref[0, pl.ds(k0, KV_TILE), ks], et_ref[pl.ds(k0, KV_TILE), :]], axis=1)
            s_ref[...] = _dot_nt(keys, lhs_aug)

        def absorb(slot, s_ref):
            k0 = pl.multiple_of(idx_ref[slot] * KV_TILE, KV_TILE)
            s_t = s_ref[...]
            m_old = m_g[...]
            m_new = jnp.maximum(m_old, jnp.max(s_t, axis=0, keepdims=True))
            p_t = jnp.exp(s_t - m_new).astype(BF16)
            pv = jnp.dot(jnp.concatenate([vsel_ref[0, :, pl.ds(k0, KV_TILE)], ones_kv], axis=0), p_t,
                         preferred_element_type=F32)
            acc_g[...] = jnp.exp(m_old - m_new) * acc_g[...] + pv
            m_g[...] = m_new

        @pl.when(count > 0)
        def _():
            scores(0, s0_ref)

        def pair_step(jj, carry):
            scores(2 * jj + 1, s1_ref)
            absorb(2 * jj, s0_ref)
            scores(jnp.minimum(2 * jj + 2, count - 1), s0_ref)
            absorb(2 * jj + 1, s1_ref)
            return carry

        lax.fori_loop(0, count // 2, pair_step, 0)

        @pl.when(count % 2 == 1)
        def _():
            absorb(count - 1, s0_ref)

        acc = acc_g[...]
        o_sel.append(acc[g * HEAD_DIM:(g + 1) * HEAD_DIM] / acc[LANES:LANES + HEAD_DIM])

    o_win = [ow_ref[g] for g in range(N_KV)]
    gates_t = gates_ref[0].T
    low = lane < HEAD_DIM
    for n in range(HPG):
        qs = slice(n * Q_TILE, (n + 1) * Q_TILE)
        halves = []
        for g in range(N_KV):
            h = g * HPG + n
            halves.append(ocmp_ref[0, 0, g * HEAD_DIM:(g + 1) * HEAD_DIM, qs]
                          + gates_t[N_HEADS + h:N_HEADS + h + 1] * o_sel[g][:, qs]
                          + gates_t[2 * N_HEADS + h:2 * N_HEADS + h + 1] * o_win[g][:, qs])
        y = jnp.concatenate(halves, axis=0).T
        y2 = y * y
        ms_lo = jnp.sum(jnp.where(low, y2, 0.0), axis=-1, keepdims=True) * (1.0 / HEAD_DIM)
        ms_hi = jnp.sum(jnp.where(low, 0.0, y2), axis=-1, keepdims=True) * (1.0 / HEAD_DIM)
        inv = jnp.where(low, lax.rsqrt(ms_lo + EPS), lax.rsqrt(ms_hi + EPS))
        y_ref[0, :, n * LANES:(n + 1) * LANES] = (
            y * inv * gattn_ref[:, n * LANES:(n + 1) * LANES]).astype(BF16)


def _sel_win(flags, q, ksel, vsel, kwin, vwin, selb, et, gates, ocmp, g_attn):
    bsz, seq, _ = q.shape
    nsb = et.shape[1]
    tile = lambda width: pl.BlockSpec((1, Q_TILE, width), lambda b, i, *_: (b, i, 0))
    whole = lambda width: pl.BlockSpec((1, seq, width), lambda b, i, *_: (b, 0, 0))
    whole_t = pl.BlockSpec((1, KV_DIM, seq), lambda b, i, *_: (b, 0, 0))
    cols = HPG * Q_TILE
    return pl.pallas_call(
        _sel_win_kernel,
        grid_spec=pltpu.PrefetchScalarGridSpec(
            num_scalar_prefetch=1,
            grid=(bsz, seq // Q_TILE),
            in_specs=[tile(D_ATTN), whole(2 * LANES), whole_t, whole(2 * LANES), whole_t,
                      tile(N_KV * nsb), _const_spec(et.shape), tile(LANES),
                      pl.BlockSpec((1, 1, KV_DIM, cols), lambda b, i, *_: (b, i, 0, 0)),
                      _const_spec(g_attn.shape)],
            out_specs=tile(D_ATTN),
            scratch_shapes=[pltpu.VMEM((N_KV, 1, cols), F32), pltpu.VMEM((N_KV, 2 * LANES, cols), F32),
                            pltpu.VMEM((N_KV, HEAD_DIM, cols), F32),
                            pltpu.VMEM((KV_TILE, cols), F32), pltpu.VMEM((KV_TILE, cols), F32),
                            pltpu.SMEM((seq // KV_TILE,), jnp.int32)]),
        out_shape=jax.ShapeDtypeStruct((bsz, seq, D_ATTN), BF16),
        compiler_params=pltpu.CompilerParams(dimension_semantics=("arbitrary", "arbitrary"),
                                             vmem_limit_bytes=VMEM_LIMIT),
        name="sel_win_attn",
    )(flags, q, ksel, vsel, kwin, vwin, selb, et, gates, ocmp, g_attn)


def _out_ffn_kernel(x_ref, yconv_ref, yattn_ref, mod_ref, woc_ref, woa_ref, gpm_ref, gpf_ref,
                    wg_ref, wu_ref, wd_ref, gpo_ref, o_ref, f_ref):
    D = D_MODEL
    ga_m = mod_ref[0, :, 2 * D:3 * D]
    sh_f = mod_ref[0, :, 3 * D:4 * D]
    sc_f = mod_ref[0, :, 4 * D:5 * D]
    ga_f = mod_ref[0, :, 5 * D:6 * D]
    y = (jnp.dot(yconv_ref[0], woc_ref[...], preferred_element_type=F32)
         + jnp.dot(yattn_ref[0], woa_ref[...], preferred_element_type=F32))
    x1 = x_ref[0] + ga_m * (_rms(y) * gpm_ref[...])
    h = (_rms(x1) * gpf_ref[...] * (1.0 + sc_f) + sh_f).astype(BF16)
    for c in range(D_FF // FF_CHUNK):
        cols = slice(c * FF_CHUNK, (c + 1) * FF_CHUNK)
        gate = jnp.dot(h, wg_ref[:, cols], preferred_element_type=F32)
        up = jnp.dot(h, wu_ref[:, cols], preferred_element_type=F32)
        act = (gate * jax.nn.sigmoid(gate) * up).astype(BF16)
        part = jnp.dot(act, wd_ref[cols, :], preferred_element_type=F32)
        if c == 0:
            f_ref[...] = part
        else:
            f_ref[...] += part
    o_ref[0] = x1 + ga_f * (_rms(f_ref[...]) * gpo_ref[...])


def _out_ffn(x, yconv, yattn, mod3, wo_c, wo_a, g_pm, g_pf, wg, wu, wd, g_po):
    bsz, seq, _ = x.shape
    tm = min(ROW_TILE, seq)
    row = lambda width: pl.BlockSpec((1, tm, width), lambda b, j: (b, j, 0))
    return pl.pallas_call(
        _out_ffn_kernel,
        grid=(bsz, seq // tm),
        in_specs=[row(D_MODEL), row(D_CONV), row(D_ATTN),
                  pl.BlockSpec((1, 1, mod3.shape[2]), lambda b, j: (b, 0, 0)),
                  _const_spec(wo_c.shape), _const_spec(wo_a.shape), _const_spec(g_pm.shape),
                  _const_spec(g_pf.shape), _const_spec(wg.shape), _const_spec(wu.shape),
                  _const_spec(wd.shape), _const_spec(g_po.shape)],
        out_specs=row(D_MODEL),
        out_shape=jax.ShapeDtypeStruct((bsz, seq, D_MODEL), F32),
        scratch_shapes=[pltpu.VMEM((tm, D_MODEL), F32)],
        compiler_params=pltpu.CompilerParams(dimension_semantics=("arbitrary", "arbitrary"),
                                             vmem_limit_bytes=VMEM_LIMIT),
        name="out_ffn",
    )(x, yconv, yattn, mod3, wo_c, wo_a, g_pm, g_pf, wg, wu, wd, g_po)


def _pair_perm():
    idx = []
    for n in range(HPG):
        for g in range(N_KV):
            base = g * HPG * HEAD_DIM + n * HEAD_DIM
            idx.extend(range(base, base + HEAD_DIM))
    return jnp.array(idx, jnp.int32)


def _gate_perm():
    return jnp.array([h * 3 + j for j in range(3) for h in range(N_HEADS)], jnp.int32)


def _expand_w1(w1_half_k, w1_half_v):
    eye_g = jnp.eye(N_KV, dtype=F32)
    blocks = []
    for kv, w in enumerate((w1_half_k, w1_half_v)):
        e = jnp.einsum("ldh,gk->lgdkh", w, eye_g).reshape(CMP_STRIDE, KV_DIM, N_KV * CMP_HIDDEN)
        pad = jnp.zeros_like(e)
        blocks.append(jnp.concatenate([e, pad] if kv == 0 else [pad, e], axis=2))
    return jnp.concatenate(blocks, axis=1).reshape(CMP_STRIDE * 2 * KV_DIM, 2 * N_KV * CMP_HIDDEN)


def _expand_pos(pos_half):
    rowv = jnp.tile(pos_half[:, None, None, :], (1, 2, N_KV, 1)).reshape(1, -1)
    return jnp.tile(rowv, (8, 1))


def _expand_w2(w2_k, w2_v):
    eye_g = jnp.eye(N_KV, dtype=F32)
    out = []
    for kv, w in enumerate((w2_k, w2_v)):
        e = jnp.einsum("hd,gk->ghkd", w, eye_g).reshape(N_KV * CMP_HIDDEN, KV_DIM)
        pad = jnp.zeros_like(e)
        out.append(jnp.concatenate([e, pad] if kv == 0 else [pad, e], axis=1))
    return jnp.concatenate(out, axis=0)


def kernel(x, c, w_mod, b_mod, g_pre_mix, w_in, conv_w, pos_cmp, w_ck1, w_ck2, w_cv1, w_cv2, g_conv_out,
           g_attn_out, w_out, g_post_mix, g_pre_ffn, w_gate, w_up, w_down, g_post_ffn):
    bsz, seq, _ = x.shape
    assert seq % KV_TILE == 0 and seq >= WIN_SPAN and w_mod.shape[0] == 1
    nsb = seq // SEL_LEN
    nch = seq // CMP_STRIDE
    l = 0
    perm = _pair_perm()

    o_q = 3 * D_CONV
    o_kv = o_q + D_ATTN
    o_gl = o_kv + 6 * KV_DIM
    w_in_l = w_in[l]
    w_conv = w_in_l[:, :o_q].astype(BF16)
    w_q = w_in_l[:, o_q:o_kv][:, perm].astype(BF16)
    w_kv = w_in_l[:, o_kv:o_gl].astype(BF16)
    w_gl = jnp.pad(w_in_l[:, o_gl:][:, _gate_perm()], ((0, 0), (0, LANES - 3 * N_HEADS))).astype(BF16)

    mod3 = _modulation(c, w_mod[l], b_mod[l]).reshape(bsz, 1, 6 * D_MODEL)
    yconv, q, cmp_kv, ksel, vsel, kwin, vwin, gates = _in_projection(
        x, mod3, g_pre_mix[l][None], w_conv, w_q, w_kv, w_gl, conv_w[l], g_conv_out[l][None])

    kc, vct = _compress(
        cmp_kv.reshape(bsz, nch, CMP_STRIDE * 2 * KV_DIM),
        _expand_w1(w_ck1[l][:CMP_STRIDE], w_cv1[l][:CMP_STRIDE]).astype(BF16),
        _expand_w1(w_ck1[l][CMP_STRIDE:], w_cv1[l][CMP_STRIDE:]).astype(BF16),
        _expand_pos(pos_cmp[l][:CMP_STRIDE]), _expand_pos(pos_cmp[l][CMP_STRIDE:]),
        _expand_w2(w_ck2[l], w_cv2[l]).astype(BF16))

    ck = jnp.arange(nch)[None, :] * CMP_STRIDE
    sj = jnp.arange(nsb)[:, None] * SEL_LEN
    ovt = ((ck < sj + SEL_LEN) & (ck + CMP_LEN > sj) & (ck < (nch - 1) * CMP_STRIDE)).astype(BF16)
    grp = (jnp.arange(nsb)[:, None] // (KV_TILE // SEL_LEN) == jnp.arange(LANES)[None, :]).astype(BF16)
    ocmp, selb, flag_rows = _cmp_select(q, kc, vct, gates, ovt, grp)
    n_words = pl.cdiv(seq // KV_TILE, FLAG_BITS)
    flags = flag_rows[:, :, :N_KV, :n_words].reshape(-1)

    et = (jnp.arange(seq)[:, None] // SEL_LEN == jnp.arange(nsb)[None, :]).astype(BF16)
    yattn = _sel_win(flags, q, ksel, vsel, kwin, vwin, selb, et, gates, ocmp, g_attn_out[l][perm][None])

    w_out_l = w_out[l]
    return _out_ffn(x, yconv, yattn, mod3, w_out_l[:D_CONV].astype(BF16), w_out_l[D_CONV:][perm].astype(BF16),
                    g_post_mix[l][None], g_pre_ffn[l][None], w_gate[l].astype(BF16), w_up[l].astype(BF16),
                    w_down[l].astype(BF16), g_post_ffn[l][None])
```
